```python
import math
import jax, jax.numpy as jnp
from jax import lax
import numpy as np

D_MODEL = 1024
BATCH = 2
SEQ = 8192
DEPTH = 2

N_EVEN = (DEPTH + 1) // 2
N_ODD = DEPTH // 2
RMS_EPS = 1e-6
LN_EPS = 1e-5

W_A = D_MODEL
CONV_K = 31
A_GROUPS = 8
H_B = 4
DQK_B = D_MODEL // (2 * H_B)
DV_B = D_MODEL // H_B
W_B = H_B * DV_B
CHUNK = 64
H_C = 8
QK_NOPE = 128
QK_ROPE = 64
V_HEAD = 128
Q_LORA = 3 * D_MODEL // 8
KV_LORA = D_MODEL // 4
W_C = H_C * V_HEAD
Q_BLOCK = 128
ROPE_THETA = 10000.0
D_GROUPS = 4
D_GROUP_CH = D_MODEL // 8
W_D = D_GROUPS * D_GROUP_CH

EVEN_SIZES = (W_A, W_A, W_A, H_B * DQK_B, H_B * DQK_B, W_B, W_B, W_B, 4 * H_B)
ODD_SIZES = (Q_LORA, KV_LORA, QK_ROPE, W_D, W_C, W_D)
P_EVEN = sum(EVEN_SIZES)
P_ODD = sum(ODD_SIZES)

kernel_name = "hybrid_conv_mlstm_mla_fnet_encoder"


def split_cols(p, sizes):
    idx = [int(i) for i in np.cumsum(sizes)[:-1]]
    return jnp.split(p, idx, axis=-1)


def rms_norm(x, g):
    xf = x.astype(jnp.float32)
    y = xf * lax.rsqrt(jnp.mean(xf * xf, axis=-1, keepdims=True) + RMS_EPS)
    return (y * g.astype(jnp.float32)).astype(x.dtype)


def group_layer_norm(x, g, b, groups):
    B, S, C = x.shape
    xf = x.astype(jnp.float32).reshape(B, S, groups, C // groups)
    mu = jnp.mean(xf, axis=-1, keepdims=True)
    xc = xf - mu
    var = jnp.mean(xc * xc, axis=-1, keepdims=True)
    y = (xc * lax.rsqrt(var + LN_EPS)).reshape(B, S, C)
    return (y * g.astype(jnp.float32) + b.astype(jnp.float32)).astype(x.dtype)


def depthwise_conv_centred(x, w, b):
    K = w.shape[0]
    y = lax.conv_general_dilated(
        x, w[:, None, :].astype(x.dtype), window_strides=(1,),
        padding=[(K // 2, K // 2)], dimension_numbers=("NWC", "WIO", "NWC"),
        feature_group_count=x.shape[-1])
    return y + b.astype(x.dtype)


def rope_tables(positions):
    inv = ROPE_THETA ** (-jnp.arange(0, QK_ROPE, 2, dtype=jnp.float32) / QK_ROPE)
    ang = positions.astype(jnp.float32)[..., None] * inv
    return jnp.cos(ang), jnp.sin(ang)


def apply_rope(x, cos, sin):
    half = x.shape[-1] // 2
    xf = x.astype(jnp.float32)
    x1, x2 = xf[..., :half], xf[..., half:]
    return jnp.concatenate([x1 * cos - x2 * sin, x1 * sin + x2 * cos], axis=-1).astype(x.dtype)


def mlstm_chunkwise(q, k, v, i_pre, f_pre):
    B, H, S, dk = q.shape
    dv = v.shape[-1]
    nc = S // CHUNK
    f32 = jnp.float32

    def chunks(a):
        return jnp.moveaxis(a.reshape(B, H, nc, CHUNK, *a.shape[3:]), 2, 0)

    qc = chunks(q.astype(f32))
    kc = chunks(k.astype(f32) * (dk ** -0.5))
    vc = chunks(v.astype(f32))
    ic = chunks(i_pre.astype(f32))
    bc = jnp.cumsum(chunks(jax.nn.log_sigmoid(f_pre.astype(f32))), axis=-1)
    tril = jnp.tril(jnp.ones((CHUNK, CHUNK), dtype=bool))

    def step(carry, inp):
        C, n, m = carry
        qb, kb, vb, ib, bb = inp
        D = jnp.where(tril, bb[..., :, None] - bb[..., None, :] + ib[..., None, :], -jnp.inf)
        a = bb + m[..., None]
        m_t = jnp.maximum(a, jnp.max(D, axis=-1))
        Dw = jnp.exp(D - m_t[..., None])
        aw = jnp.exp(a - m_t)
        s = jnp.einsum('bhtd,bhsd->bhts', qb, kb) * Dw
        num = jnp.einsum('bhts,bhsv->bhtv', s, vb) + aw[..., None] * jnp.einsum('bhtd,bhdv->bhtv', qb, C)
        den = jnp.sum(s, axis=-1) + aw * jnp.einsum('bhtd,bhd->bht', qb, n)
        h = num / jnp.maximum(jnp.abs(den), jnp.exp(-m_t))[..., None]
        bL = bb[..., -1]
        g = bL[..., None] - bb + ib
        m_new = jnp.maximum(bL + m, jnp.max(g, axis=-1))
        wk = jnp.exp(g - m_new[..., None])
        decay = jnp.exp(bL + m - m_new)
        C_new = decay[..., None, None] * C + jnp.einsum('bhs,bhsd,bhsv->bhdv', wk, kb, vb)
        n_new = decay[..., None] * n + jnp.einsum('bhs,bhsd->bhd', wk, kb)
        return (C_new, n_new, m_new), h

    init = (jnp.zeros((B, H, dk, dv), f32), jnp.zeros((B, H, dk), f32), jnp.zeros((B, H), f32))
    _, hs = lax.scan(step, init, (qc, kc, vc, ic, bc))
    return jnp.moveaxis(hs, 0, 2).reshape(B, H, S, dv)


def even_layer(x, g_pre, g_post, w_in, b_gate, conv_w, conv_b, gn_g, gn_b, hn_g, w_out):
    B, S, _ = x.shape
    h = rms_norm(x, g_pre)
    p = h @ w_in
    a_val, a_gate, z_a, q, k, v, o, z_b, gates = split_cols(p, EVEN_SIZES)
    u = a_val * jax.nn.sigmoid(a_gate)
    u = depthwise_conv_centred(u, conv_w, conv_b)
    u = jax.nn.silu(group_layer_norm(u, gn_g, gn_b, A_GROUPS))
    y_a = u * jax.nn.silu(z_a)
    to_heads = lambda t, d: t.reshape(B, S, H_B, d).transpose(0, 2, 1, 3)
    qh, kh, vh = to_heads(q, DQK_B), to_heads(k, DQK_B), to_heads(v, DV_B)
    gt = (gates.astype(jnp.float32) + b_gate.astype(jnp.float32)).reshape(B, S, 4, H_B).transpose(2, 0, 3, 1)
    i_f, f_f, i_b, f_b = gt[0], gt[1], gt[2], gt[3]
    h_fwd = mlstm_chunkwise(qh, kh, vh, i_f, f_f)
    fl = lambda t: jnp.flip(t, axis=2)
    h_bwd = fl(mlstm_chunkwise(fl(qh), fl(kh), fl(vh), fl(i_b), fl(f_b)))
    hm = (h_fwd + h_bwd).transpose(0, 2, 1, 3)
    hm = rms_norm(hm, hn_g.reshape(H_B, DV_B)).astype(x.dtype)
    hm = jax.nn.sigmoid(o.reshape(B, S, H_B, DV_B)) * hm
    y_b = hm.reshape(B, S, W_B) * jax.nn.silu(z_b)
    y = jnp.concatenate([y_a, y_b], axis=-1) @ w_out
    return x + rms_norm(y, g_post)


def odd_layer(x, cos, sin, g_pre, g_post, w_in, g_q, w_uq, g_kv, w_ukv, w_fd, w_out):
    B, S, _ = x.shape
    h = rms_norm(x, g_pre)
    p = h @ w_in
    c_q, c_kv, k_r, f_in, z_c, z_d = split_cols(p, ODD_SIZES)
    qh = (rms_norm(c_q, g_q) @ w_uq).reshape(B, S, H_C, QK_NOPE + QK_ROPE)
    q_nope = qh[..., :QK_NOPE]
    q_rope = apply_rope(qh[..., QK_NOPE:], cos[:, :, None, :], sin[:, :, None, :])
    kv = (rms_norm(c_kv, g_kv) @ w_ukv).reshape(B, S, H_C, QK_NOPE + V_HEAD)
    k_nope, v = kv[..., :QK_NOPE], kv[..., QK_NOPE:]
    k_rope = apply_rope(k_r, cos, sin)
    scale = (QK_NOPE + QK_ROPE) ** -0.5
    nb = S // Q_BLOCK
    blk = lambda t: jnp.moveaxis(t.reshape(B, nb, Q_BLOCK, *t.shape[2:]), 1, 0)

    def attend(qs):
        qn, qr = qs
        s = jnp.einsum('bqhd,bkhd->bhqk', qn, k_nope) + jnp.einsum('bqhr,bkr->bhqk', qr, k_rope)
        pr = jax.nn.softmax(s.astype(jnp.float32) * scale, axis=-1).astype(v.dtype)
        return jnp.einsum('bhqk,bkhd->bqhd', pr, v)

    att = lax.map(attend, (blk(q_nope), blk(q_rope)))
    att = jnp.moveaxis(att, 0, 1).reshape(B, S, W_C)
    y_c = att * jax.nn.silu(z_c)
    fg = f_in.astype(jnp.float32).reshape(B, S, D_GROUPS, D_GROUP_CH)
    fr = jnp.real(jnp.fft.fft2(fg, axes=(1, 3), norm="ortho")).astype(x.dtype).reshape(B, S, W_D)
    y_d = (fr @ w_fd) * jax.nn.silu(z_d)
    y = jnp.concatenate([y_c, y_d], axis=-1) @ w_out
    return x + rms_norm(y, g_post)


def setup_inputs(seed: int = 0) -> dict:
    key = jax.random.key(seed)
    ks = jax.random.split(key, 24)
    f32 = jnp.float32
    nrm = lambda k, shape, fan_in: jax.random.normal(k, shape, f32) * (fan_in ** -0.5)
    gain = lambda k, shape: 1.0 + 0.05 * jax.random.normal(k, shape, f32)
    small = lambda k, shape: 0.02 * jax.random.normal(k, shape, f32)
    x = jax.random.normal(ks[0], (BATCH, SEQ, D_MODEL), f32)
    positions = (jnp.arange(SEQ, dtype=jnp.int32)[None, :]
                 + jax.random.randint(ks[1], (BATCH, 1), 0, 1024, dtype=jnp.int32))
    fbias = jnp.linspace(3.0, 6.0, H_B, dtype=f32)
    gate_base = jnp.concatenate([jnp.zeros((H_B,), f32), fbias, jnp.zeros((H_B,), f32), fbias])
    even_b_gate = gate_base[None, :] + 0.1 * jax.random.normal(ks[2], (N_EVEN, 4 * H_B), f32)
    return {
        "x": x,
        "positions": positions,
        "even_g_pre": gain(ks[3], (N_EVEN, D_MODEL)),
        "even_g_post": gain(ks[4], (N_EVEN, D_MODEL)),
        "even_w_in": nrm(ks[5], (N_EVEN, D_MODEL, P_EVEN), D_MODEL),
        "even_b_gate": even_b_gate,
        "even_conv_w": nrm(ks[6], (N_EVEN, CONV_K, W_A), CONV_K),
        "even_conv_b": small(ks[7], (N_EVEN, W_A)),
        "even_gn_g": gain(ks[8], (N_EVEN, W_A)),
        "even_gn_b": small(ks[9], (N_EVEN, W_A)),
        "even_hn_g": gain(ks[10], (N_EVEN, W_B)),
        "even_w_out": nrm(ks[11], (N_EVEN, W_A + W_B, D_MODEL), W_A + W_B),
        "odd_g_pre": gain(ks[12], (N_ODD, D_MODEL)),
        "odd_g_post": gain(ks[13], (N_ODD, D_MODEL)),
        "odd_w_in": nrm(ks[14], (N_ODD, D_MODEL, P_ODD), D_MODEL),
        "odd_g_q": gain(ks[15], (N_ODD, Q_LORA)),
        "odd_w_uq": nrm(ks[16], (N_ODD, Q_LORA, H_C * (QK_NOPE + QK_ROPE)), Q_LORA),
        "odd_g_kv": gain(ks[17], (N_ODD, KV_LORA)),
        "odd_w_ukv": nrm(ks[18], (N_ODD, KV_LORA, H_C * (QK_NOPE + V_HEAD)), KV_LORA),
        "odd_w_fd": nrm(ks[19], (N_ODD, W_D, W_D), W_D),
        "odd_w_out": nrm(ks[20], (N_ODD, W_C + W_D, D_MODEL), W_C + W_D),
    }


def reference(x, positions, even_g_pre, even_g_post, even_w_in, even_b_gate, even_conv_w,
              even_conv_b, even_gn_g, even_gn_b, even_hn_g, even_w_out, odd_g_pre, odd_g_post,
              odd_w_in, odd_g_q, odd_w_uq, odd_g_kv, odd_w_ukv, odd_w_fd, odd_w_out):
    cos, sin = rope_tables(positions)
    h = x
    for layer in range(DEPTH):
        j = layer // 2
        if layer % 2 == 0:
            h = even_layer(h, even_g_pre[j], even_g_post[j], even_w_in[j], even_b_gate[j],
                           even_conv_w[j], even_conv_b[j], even_gn_g[j], even_gn_b[j],
                           even_hn_g[j], even_w_out[j])
        else:
            h = odd_layer(h, cos, sin, odd_g_pre[j], odd_g_post[j], odd_w_in[j], odd_g_q[j],
                          odd_w_uq[j], odd_g_kv[j], odd_w_ukv[j], odd_w_fd[j], odd_w_out[j])
    return h
```

```python
import functools
import math

import numpy as np
import jax
import jax.numpy as jnp
from jax import lax
from jax.experimental import pallas as pl
from jax.experimental.pallas import tpu as pltpu

F32 = jnp.float32
BF16 = jnp.bfloat16

D_MODEL = 1024
RMS_EPS = 1e-6
LN_EPS = 1e-5
W_A = 1024
CONV_K = 31
A_GROUPS = 8
H_B = 4
DQK_B = 128
DV_B = 256
W_B = 1024
H_C = 8
QK_NOPE = 128
QK_ROPE = 64
V_HEAD = 128
Q_LORA = 384
KV_LORA = 256
W_C = 1024
ROPE_THETA = 10000.0
D_GROUPS = 4
D_GROUP_CH = 128
W_D = 512

LANES = 128
HALO = 16
QK_PAD = 256
VMEM_LIMIT = 56 * 1024 * 1024

MLSTM_CHUNK = 256
NEG_BIG = -1e30


def _params(sem, vmem=VMEM_LIMIT):
    return pltpu.CompilerParams(dimension_semantics=sem, vmem_limit_bytes=vmem)


def _sigmoid(x):
    return 1.0 / (1.0 + jnp.exp(-x))


def _silu(x):
    return x * _sigmoid(x)


def _log_sigmoid(x):
    return jnp.minimum(x, 0.0) - jnp.log(1.0 + jnp.exp(-jnp.abs(x)))


def _rms(x, g):
    return x * lax.rsqrt(jnp.mean(x * x, axis=-1, keepdims=True) + RMS_EPS) * g


def _dot(a, b):
    return jnp.dot(a, b, preferred_element_type=F32)


def _dot_nt(a, b):
    return lax.dot_general(a, b, (((1,), (1,)), ((), ())), preferred_element_type=F32)


def _dot_exact(a, b):
    return jnp.dot(a, b, preferred_element_type=F32, precision=lax.Precision.HIGHEST)


def _even_in_kernel(x_ref, g_ref, w_ref, wkt_ref, wg_ref, bg_ref, p_ref, kt_ref, gates_ref, h_scr):
    j = pl.program_id(1)

    @pl.when(j == 0)
    def _():
        h = _rms(x_ref[...], g_ref[...]).astype(BF16)
        h_scr[...] = h
        kt_ref[...] = _dot_nt(wkt_ref[...], h).astype(BF16)
        gates_ref[...] = _dot(h, wg_ref[...]) + bg_ref[...]

    p_ref[...] = _dot(h_scr[...], w_ref[...]).astype(BF16)


def _even_in(x2, g_pre, w_main, w_kt, w_gate, b_gate, tm, tn):
    T = x2.shape[0]
    n_main = w_main.shape[1]
    return pl.pallas_call(
        _even_in_kernel,
        grid=(T // tm, n_main // tn),
        in_specs=[
            pl.BlockSpec((tm, D_MODEL), lambda i, j: (i, 0)),
            pl.BlockSpec((1, D_MODEL), lambda i, j: (0, 0)),
            pl.BlockSpec((D_MODEL, tn), lambda i, j: (0, j)),
            pl.BlockSpec((H_B * DQK_B, D_MODEL), lambda i, j: (0, 0)),
            pl.BlockSpec((D_MODEL, LANES), lambda i, j: (0, 0)),
            pl.BlockSpec((1, LANES), lambda i, j: (0, 0)),
        ],
        out_specs=[
            pl.BlockSpec((tm, tn), lambda i, j: (i, j)),
            pl.BlockSpec((H_B * DQK_B, tm), lambda i, j: (0, i)),
            pl.BlockSpec((tm, LANES), lambda i, j: (i, 0)),
        ],
        out_shape=[
            jax.ShapeDtypeStruct((T, n_main), BF16),
            jax.ShapeDtypeStruct((H_B * DQK_B, T), BF16),
            jax.ShapeDtypeStruct((T, LANES), F32),
        ],
        scratch_shapes=[pltpu.VMEM((tm, D_MODEL), BF16)],
        compiler_params=_params(("arbitrary", "arbitrary")),
        name="even_in_proj",
    )(x2, g_pre, w_main, w_kt, w_gate, b_gate)


def _conv_kernel(av_ref, ag_ref, za_ref, avp_ref, agp_ref, avn_ref, agn_ref,
                 cw_ref, cb_ref, gg_ref, gb_ref, out_ref, u_scr, *, ts, rc):
    i = pl.program_id(1)
    last = pl.num_programs(1) - 1

    def gated(a_ref, g_ref):
        return a_ref[...].astype(F32) * _sigmoid(g_ref[...].astype(F32))

    u_scr[HALO:HALO + ts, :] = gated(av_ref, ag_ref)
    u_scr[0:HALO, :] = jnp.where(i > 0, gated(avp_ref, agp_ref), 0.0)
    u_scr[HALO + ts:HALO + ts + HALO, :] = jnp.where(i < last, gated(avn_ref, agn_ref), 0.0)

    first_tap = HALO - CONV_K // 2
    for g in range(A_GROUPS):
        cs = slice(g * LANES, (g + 1) * LANES)
        for r0 in range(0, ts, rc):
            acc = jnp.zeros((rc, LANES), F32) + cb_ref[:, cs]
            for j in range(CONV_K):
                lo = r0 + first_tap + j
                acc = acc + u_scr[lo:lo + rc, cs] * cw_ref[j:j + 1, cs]
            mu = jnp.mean(acc, axis=-1, keepdims=True)
            xc = acc - mu
            var = jnp.mean(xc * xc, axis=-1, keepdims=True)
            y = xc * lax.rsqrt(var + LN_EPS) * gg_ref[:, cs] + gb_ref[:, cs]
            y = _silu(y) * _silu(za_ref[r0:r0 + rc, cs].astype(F32))
            out_ref[r0:r0 + rc, cs] = y.astype(BF16)


def _conv_module(p, conv_w, conv_b, gn_g, gn_b, B, S, ts):
    T = B * S
    nt = S // ts
    hb = ts // HALO
    n_hblk = T // HALO

    def cur(col):
        return pl.BlockSpec((ts, W_A), lambda b, i: (b * nt + i, col))

    def prev(col):
        return pl.BlockSpec((HALO, W_A), lambda b, i: (jnp.maximum((b * nt + i) * hb - 1, 0), col))

    def nxt(col):
        return pl.BlockSpec((HALO, W_A), lambda b, i: (jnp.minimum((b * nt + i + 1) * hb, n_hblk - 1), col))

    def full(r):
        return pl.BlockSpec((r, W_A), lambda b, i: (0, 0))

    return pl.pallas_call(
        functools.partial(_conv_kernel, ts=ts, rc=min(ts, 128)),
        grid=(B, nt),
        in_specs=[cur(0), cur(1), cur(2), prev(0), prev(1), nxt(0), nxt(1),
                  full(CONV_K), full(1), full(1), full(1)],
        out_specs=pl.BlockSpec((ts, W_A), lambda b, i: (b * nt + i, 0)),
        out_shape=jax.ShapeDtypeStruct((T, W_A), BF16),
        scratch_shapes=[pltpu.VMEM((ts + 2 * HALO, W_A), F32)],
        compiler_params=_params(("arbitrary", "arbitrary")),
        name="conv_module",
    )(p, p, p, p, p, p, p, conv_w, conv_b, gn_g, gn_b)


def _mlstm_kernel(qf_ref, kf_ref, ktf_ref, vf_ref, gcf_ref, grf_ref,
                  qb_ref, kb_ref, ktb_ref, vb_ref, gcb_ref, grb_ref,
                  hf_ref, hb_ref, c_scr, n_scr, m_scr, *, L):
    c = pl.program_id(1)

    @pl.when(c == 0)
    def _():
        c_scr[...] = jnp.zeros_like(c_scr)
        n_scr[...] = jnp.zeros_like(n_scr)
        m_scr[...] = jnp.zeros_like(m_scr)

    row = lax.broadcasted_iota(jnp.int32, (L, L), 0)
    col = lax.broadcasted_iota(jnp.int32, (L, L), 1)
    lower = col <= row
    upper = col >= row
    lower_f = lower.astype(F32)
    upper_f = upper.astype(F32)
    scale = DQK_B ** -0.5

    def direction(q_ref, k_ref, kt_ref, v_ref, gc_ref, gr_ref, h_ref, fwd):
        gate_i = 0 if fwd else 2 * H_B
        gate_f = gate_i + H_B
        mask = lower if fwd else upper
        ls_col = _log_sigmoid(gc_ref[...])
        ls_row = _log_sigmoid(gr_ref[...])
        b_col_all = _dot_exact(lower_f if fwd else upper_f, ls_col)
        b_row_all = _dot_exact(ls_row, upper_f if fwd else lower_f)
        gc = gc_ref[...]
        gr = gr_ref[...]
        for h in range(H_B):
            sidx = h if fwd else H_B + h
            b_col = b_col_all[:, gate_f + h:gate_f + h + 1]
            b_row = b_row_all[gate_f + h:gate_f + h + 1, :]
            i_col = gc[:, gate_i + h:gate_i + h + 1]
            i_row = gr[gate_i + h:gate_i + h + 1, :]
            q = (q_ref[:, h * DQK_B:(h + 1) * DQK_B].astype(F32) * scale).astype(BF16)
            k = k_ref[:, h * DQK_B:(h + 1) * DQK_B]
            kt = kt_ref[h * DQK_B:(h + 1) * DQK_B, :]
            v = v_ref[:, h * DV_B:(h + 1) * DV_B]
            m = m_scr[sidx][:, 0:1]
            cst = c_scr[sidx]
            nst = n_scr[sidx]

            dmat = jnp.where(mask, b_col - b_row + i_row, -jnp.inf)
            a = b_col + m
            m_t = jnp.maximum(a, jnp.max(dmat, axis=1, keepdims=True))
            dw = jnp.exp(dmat - m_t)
            aw = jnp.exp(a - m_t)
            s = _dot(q, kt) * dw
            num = _dot(s.astype(BF16), v) + aw * _dot(q, cst.astype(BF16))
            den = jnp.sum(s, axis=1, keepdims=True) + aw * jnp.sum(q.astype(F32) * nst, axis=1, keepdims=True)
            hval = num / jnp.maximum(jnp.abs(den), jnp.exp(-m_t))
            h_ref[:, h * DV_B:(h + 1) * DV_B] = hval.astype(BF16)

            b_end = b_col[L - 1:L, :] if fwd else b_col[0:1, :]
            gl = b_end - b_col + i_col
            m_new = jnp.maximum(b_end + m, jnp.max(gl, axis=0, keepdims=True))
            wk = jnp.exp(gl - m_new)
            decay = jnp.exp(b_end + m - m_new)
            c_scr[sidx] = decay * cst + _dot(kt, (wk * v.astype(F32)).astype(BF16))
            n_scr[sidx] = decay * nst + jnp.sum(wk * k.astype(F32), axis=0, keepdims=True)
            m_scr[sidx] = jnp.broadcast_to(m_new, (1, LANES))

    direction(qf_ref, kf_ref, ktf_ref, vf_ref, gcf_ref, grf_ref, hf_ref, True)
    direction(qb_ref, kb_ref, ktb_ref, vb_ref, gcb_ref, grb_ref, hb_ref, False)


def _mlstm(p, kt, gates, gates_t, B, S, L):
    T = B * S
    nc = S // L
    qcol = 3 * W_A // (H_B * DQK_B)
    vcol = (3 * W_A + 2 * H_B * DQK_B) // W_B

    def specs(chunk):
        return [
            pl.BlockSpec((L, H_B * DQK_B), lambda b, c: (b * nc + chunk(c), qcol)),
            pl.BlockSpec((L, H_B * DQK_B), lambda b, c: (b * nc + chunk(c), qcol + 1)),
            pl.BlockSpec((H_B * DQK_B, L), lambda b, c: (0, b * nc + chunk(c))),
            pl.BlockSpec((L, W_B), lambda b, c: (b * nc + chunk(c), vcol)),
            pl.BlockSpec((L, LANES), lambda b, c: (b * nc + chunk(c), 0)),
            pl.BlockSpec((None, 4 * H_B, L), lambda b, c: (b, 0, chunk(c))),
        ]

    fwd = lambda c: c
    bwd = lambda c: nc - 1 - c
    n_streams = 2 * H_B
    return pl.pallas_call(
        functools.partial(_mlstm_kernel, L=L),
        grid=(B, nc),
        in_specs=specs(fwd) + specs(bwd),
        out_specs=[
            pl.BlockSpec((L, W_B), lambda b, c: (b * nc + c, 0)),
            pl.BlockSpec((L, W_B), lambda b, c: (b * nc + nc - 1 - c, 0)),
        ],
        out_shape=[jax.ShapeDtypeStruct((T, W_B), BF16)] * 2,
        scratch_shapes=[
            pltpu.VMEM((n_streams, DQK_B, DV_B), F32),
            pltpu.VMEM((n_streams, 1, DQK_B), F32),
            pltpu.VMEM((n_streams, 1, LANES), F32),
        ],
        compiler_params=_params(("arbitrary", "arbitrary")),
        name="mlstm",
    )(p, p, kt, p, gates, gates_t, p, p, kt, p, gates, gates_t)


def _even_out_kernel(ya_ref, hf_ref, hb_ref, o_ref, zb_ref, x_ref, hn_ref, w_ref, g_ref, out_ref):
    parts = []
    for h in range(H_B):
        cs = slice(h * DV_B, (h + 1) * DV_B)
        hm = hf_ref[:, cs].astype(F32) + hb_ref[:, cs].astype(F32)
        hm = _rms(hm, hn_ref[:, cs])
        yb = _sigmoid(o_ref[:, cs].astype(F32)) * hm * _silu(zb_ref[:, cs].astype(F32))
        parts.append(yb.astype(BF16))
    y = _dot(ya_ref[...], w_ref[0:W_A, :])
    for h in range(H_B):
        y = y + _dot(parts[h], w_ref[W_A + h * DV_B:W_A + (h + 1) * DV_B, :])
    out_ref[...] = x_ref[...] + _rms(y, g_ref[...])


def _even_out(ya, hf, hb, p, x2, hn_g, w_out, g_post, tm):
    T = x2.shape[0]
    ocol = (3 * W_A + 2 * H_B * DQK_B + W_B) // W_B
    row = lambda w: pl.BlockSpec((tm, w), lambda i: (i, 0))
    return pl.pallas_call(
        _even_out_kernel,
        grid=(T // tm,),
        in_specs=[row(W_A), row(W_B), row(W_B),
                  pl.BlockSpec((tm, W_B), lambda i: (i, ocol)),
                  pl.BlockSpec((tm, W_B), lambda i: (i, ocol + 1)),
                  row(D_MODEL),
                  pl.BlockSpec((1, W_B), lambda i: (0, 0)),
                  pl.BlockSpec((W_A + W_B, D_MODEL), lambda i: (0, 0)),
                  pl.BlockSpec((1, D_MODEL), lambda i: (0, 0))],
        out_specs=row(D_MODEL),
        out_shape=jax.ShapeDtypeStruct((T, D_MODEL), F32),
        compiler_params=_params(("arbitrary",)),
        name="even_out_proj",
    )(ya, hf, hb, p, p, x2, hn_g, w_out, g_post)


def _rope128(x, cc, ss):
    lane = lax.broadcasted_iota(jnp.int32, x.shape, 1)
    swapped = jnp.where(lane % QK_ROPE < QK_ROPE // 2,
                        pltpu.roll(x, LANES - QK_ROPE // 2, 1),
                        pltpu.roll(x, QK_ROPE // 2, 1))
    return x * cc + swapped * ss


def _odd_in_kernel(x_ref, g_ref, w_ref, gq_ref, wuq_ref, gkv_ref, wukv_ref, cc_ref, ss_ref,
                   q_ref, k_ref, v_ref, f_ref, zc_ref, zd_ref):
    tm = x_ref.shape[0]
    h = _rms(x_ref[...], g_ref[...]).astype(BF16)
    o_ckv = Q_LORA
    o_f = o_ckv + KV_LORA
    o_zc = o_f + W_D
    o_zd = o_zc + W_C
    o_kr = o_zd + W_D
    cc = cc_ref[...]
    ss = ss_ref[...]
    lane = lax.broadcasted_iota(jnp.int32, (tm, LANES), 1)
    first_half = lane < QK_ROPE
    zeros_pad = jnp.zeros((tm, LANES), BF16)
    scale = (QK_NOPE + QK_ROPE) ** -0.5

    for g in range(D_GROUPS):
        f_ref[g] = _dot(h, w_ref[:, o_f + g * D_GROUP_CH:o_f + (g + 1) * D_GROUP_CH]).astype(BF16)
    zc_ref[...] = _dot(h, w_ref[:, o_zc:o_zc + W_C]).astype(BF16)
    zd_ref[...] = _dot(h, w_ref[:, o_zd:o_zd + W_D]).astype(BF16)

    ckv = _rms(_dot(h, w_ref[:, o_ckv:o_ckv + KV_LORA]), gkv_ref[...]).astype(BF16)
    kr = _rope128(_dot(h, w_ref[:, o_kr:o_kr + LANES]), cc, ss)
    kr_even = jnp.where(first_half, kr, 0.0).astype(BF16)
    kr_odd = jnp.where(first_half, 0.0, kr).astype(BF16)
    for hd in range(H_C):
        k_ref[hd, :, 0:QK_NOPE] = _dot(ckv, wukv_ref[:, hd * QK_NOPE:(hd + 1) * QK_NOPE]).astype(BF16)
        k_ref[hd, :, QK_NOPE:QK_PAD] = kr_even if hd % 2 == 0 else kr_odd
        vo = H_C * QK_NOPE + hd * V_HEAD
        v_ref[hd] = _dot(ckv, wukv_ref[:, vo:vo + V_HEAD]).astype(BF16)

    cq = _rms(_dot(h, w_ref[:, 0:Q_LORA]), gq_ref[...]).astype(BF16)
    for hd in range(H_C):
        q_ref[hd, :, 0:QK_NOPE] = (_dot(cq, wuq_ref[:, hd * QK_NOPE:(hd + 1) * QK_NOPE]) * scale).astype(BF16)
    for pair in range(H_C // 2):
        ro = H_C * QK_NOPE + pair * LANES
        qr = _rope128(_dot(cq, wuq_ref[:, ro:ro + LANES]), cc, ss) * scale
        q_ref[2 * pair, :, QK_NOPE:QK_PAD] = jnp.where(first_half, qr, 0.0).astype(BF16)
        q_ref[2 * pair + 1, :, QK_NOPE:QK_PAD] = jnp.where(first_half, 0.0, qr).astype(BF16)
    del zeros_pad


def _odd_in(x2, g_pre, w_in, g_q, w_uq, g_kv, w_ukv, cc, ss, B, S, tm):
    T = B * S
    nt = S // tm
    n_in = w_in.shape[1]
    full = lambda a: pl.BlockSpec(a.shape, lambda i: (0,) * a.ndim)
    row = lambda w: pl.BlockSpec((tm, w), lambda i: (i, 0))
    head = lambda w: pl.BlockSpec((None, H_C, tm, w), lambda i: (i // nt, 0, i % nt, 0))
    return pl.pallas_call(
        _odd_in_kernel,
        grid=(T // tm,),
        in_specs=[row(D_MODEL), full(g_pre), full(w_in), full(g_q), full(w_uq), full(g_kv), full(w_ukv),
                  row(LANES), row(LANES)],
        out_specs=[head(QK_PAD), head(QK_PAD), head(V_HEAD),
                   pl.BlockSpec((D_GROUPS, None, tm, D_GROUP_CH), lambda i: (0, i // nt, i % nt, 0)),
                   row(W_C), row(W_D)],
        out_shape=[
            jax.ShapeDtypeStruct((B, H_C, S, QK_PAD), BF16),
            jax.ShapeDtypeStruct((B, H_C, S, QK_PAD), BF16),
            jax.ShapeDtypeStruct((B, H_C, S, V_HEAD), BF16),
            jax.ShapeDtypeStruct((D_GROUPS, B, S, D_GROUP_CH), BF16),
            jax.ShapeDtypeStruct((T, W_C), BF16),
            jax.ShapeDtypeStruct((T, W_D), BF16),
        ],
        compiler_params=_params(("arbitrary",)),
        name="odd_in_proj",
    )(x2, g_pre, w_in, g_q, w_uq, g_kv, w_ukv, cc, ss)


def _attn_kernel(q_ref, k_ref, v_ref, o_ref, *, tk):
    tq = q_ref.shape[0]
    S = k_ref.shape[0]
    q = q_ref[...]

    def body(j, carry):
        m, l, acc = carry
        start = pl.multiple_of(j * tk, tk)
        k = k_ref[pl.ds(start, tk), :]
        v = v_ref[pl.ds(start, tk), :]
        s = _dot_nt(q, k)
        m_new = jnp.maximum(m, jnp.max(s, axis=1, keepdims=True))
        alpha = jnp.exp(m - m_new)
        p = jnp.exp(s - m_new)
        l = alpha * l + jnp.sum(p, axis=1, keepdims=True)
        acc = alpha * acc + _dot(p.astype(BF16), v)
        return m_new, l, acc

    init = (jnp.full((tq, 1), NEG_BIG, F32), jnp.zeros((tq, 1), F32), jnp.zeros((tq, V_HEAD), F32))
    m, l, acc = lax.fori_loop(0, S // tk, body, init)
    o_ref[...] = (acc / l).astype(BF16)


def _attention(q, k, v, B, S, tq, tk):
    return pl.pallas_call(
        functools.partial(_attn_kernel, tk=tk),
        grid=(B, H_C, S // tq),
        in_specs=[
            pl.BlockSpec((None, None, tq, QK_PAD), lambda b, h, i: (b, h, i, 0)),
            pl.BlockSpec((None, None, S, QK_PAD), lambda b, h, i: (b, h, 0, 0)),
            pl.BlockSpec((None, None, S, V_HEAD), lambda b, h, i: (b, h, 0, 0)),
        ],
        out_specs=pl.BlockSpec((None, tq, V_HEAD), lambda b, h, i: (b, i, h)),
        out_shape=jax.ShapeDtypeStruct((B, S, W_C), BF16),
        compiler_params=_params(("arbitrary", "arbitrary", "arbitrary")),
        name="flash_attention",
    )(q, k, v)


def _fft_tables(S):
    n1 = S // LANES
    a = np.arange(n1, dtype=np.float64)
    ang1 = 2.0 * np.pi * np.outer(a, a) / n1
    w1 = np.concatenate([np.cos(ang1), -np.sin(ang1)], axis=0)
    k1 = np.arange(n1)[:, None, None]
    k2 = np.arange(LANES)[None, :, None]
    n2 = np.arange(LANES)[None, None, :]
    kk = (k1 + n1 * k2) * n2 % S
    ang2 = 2.0 * np.pi * kk.astype(np.float64) / S
    gc = np.cos(ang2).reshape(n1 * LANES, LANES)
    gs = np.sin(ang2).reshape(n1 * LANES, LANES)
    c = np.arange(D_GROUP_CH, dtype=np.float64)
    angc = 2.0 * np.pi * np.outer(c, c) / D_GROUP_CH
    f32 = lambda t: jnp.asarray(t.astype(np.float32))
    return f32(w1), f32(gc), f32(gs), f32(np.cos(angc)), f32(np.sin(angc))


def _fft_kernel(x_ref, w1_ref, gc_ref, gs_ref, cc_ref, sc_ref, out_ref, ar_scr, ai_scr, *, n1, cw):
    n_chunks = (LANES * LANES) // cw
    per = cw // LANES
    w1 = w1_ref[...]

    def stage1(ch, carry):
        lo = pl.multiple_of(ch * cw, cw)
        res = _dot(w1, x_ref[:, pl.ds(lo, cw)])
        for j in range(per):
            r0 = pl.multiple_of((ch * per + j) * n1, n1)
            ar_scr[pl.ds(r0, n1), :] = res[0:n1, j * LANES:(j + 1) * LANES]
            ai_scr[pl.ds(r0, n1), :] = res[n1:2 * n1, j * LANES:(j + 1) * LANES]
        return carry

    lax.fori_loop(0, n_chunks, stage1, 0)

    ccm = cc_ref[...]
    scm = sc_ref[...]
    norm = 1.0 / math.sqrt(n1 * LANES * D_GROUP_CH)

    def stage2(k1, carry):
        ar = ar_scr[pl.ds(k1, LANES, stride=n1), :]
        ai = ai_scr[pl.ds(k1, LANES, stride=n1), :]
        a = jnp.concatenate([ar, ai], axis=1).astype(BF16)
        t0 = pl.multiple_of(k1 * LANES, LANES)
        p1 = _dot(gc_ref[pl.ds(t0, LANES), :], a)
        p2 = _dot(gs_ref[pl.ds(t0, LANES), :], a)
        zr = p1[:, 0:LANES] + p2[:, LANES:2 * LANES]
        zi = p1[:, LANES:2 * LANES] - p2[:, 0:LANES]
        fr = _dot(zr.astype(BF16), ccm) + _dot(zi.astype(BF16), scm)
        out_ref[pl.ds(k1, LANES, stride=n1), :] = fr * norm
        return carry

    lax.fori_loop(0, n1, stage2, 0)


def _fft(f4, B, S):
    n1 = S // LANES
    w1, gc, gs, cc, sc = _fft_tables(S)
    bf = lambda t: t.astype(BF16)
    x4 = f4.reshape(D_GROUPS, B, n1, LANES * D_GROUP_CH)
    cw = 2048
    full = lambda a: pl.BlockSpec(a.shape, lambda g, b: (0,) * a.ndim)
    tabs = [bf(w1), bf(gc), bf(gs), bf(cc), bf(sc)]
    return pl.pallas_call(
        functools.partial(_fft_kernel, n1=n1, cw=cw),
        grid=(D_GROUPS, B),
        in_specs=[pl.BlockSpec((None, None, n1, LANES * D_GROUP_CH), lambda g, b: (g, b, 0, 0))]
                 + [full(t) for t in tabs],
        out_specs=pl.BlockSpec((None, S, D_GROUP_CH), lambda g, b: (b, 0, g)),
        out_shape=jax.ShapeDtypeStruct((B, S, W_D), F32),
        scratch_shapes=[pltpu.VMEM((S, D_GROUP_CH), F32), pltpu.VMEM((S, D_GROUP_CH), F32)],
        compiler_params=_params(("arbitrary", "arbitrary")),
        name="fft2_real",
    )(x4, *tabs)


def _odd_out_kernel(att_ref, zc_ref, fr_ref, zd_ref, x_ref, wfd_ref, w_ref, g_ref, out_ref):
    yc = (att_ref[...].astype(F32) * _silu(zc_ref[...].astype(F32))).astype(BF16)
    yd = (_dot(fr_ref[...].astype(BF16), wfd_ref[...]) * _silu(zd_ref[...].astype(F32))).astype(BF16)
    y = _dot(yc, w_ref[0:W_C, :]) + _dot(yd, w_ref[W_C:W_C + W_D, :])
    out_ref[...] = x_ref[...] + _rms(y, g_ref[...])


def _odd_out(att, zc, fr, zd, x2, w_fd, w_out, g_post, tm):
    T = x2.shape[0]
    row = lambda w: pl.BlockSpec((tm, w), lambda i: (i, 0))
    full = lambda a: pl.BlockSpec(a.shape, lambda i: (0,) * a.ndim)
    return pl.pallas_call(
        _odd_out_kernel,
        grid=(T // tm,),
        in_specs=[row(W_C), row(W_C), row(W_D), row(W_D), row(D_MODEL), full(w_fd), full(w_out), full(g_post)],
        out_specs=row(D_MODEL),
        out_shape=jax.ShapeDtypeStruct((T, D_MODEL), F32),
        compiler_params=_params(("arbitrary",)),
        name="odd_out_proj",
    )(att, zc, fr, zd, x2, w_fd, w_out, g_post)


def _even_layer(x2, B, S, g_pre, g_post, w_in, b_gate, conv_w, conv_b, gn_g, gn_b, hn_g, w_out):
    n_main = 3 * W_A + 2 * H_B * DQK_B + 3 * W_B
    k_lo = 3 * W_A + H_B * DQK_B
    w_main = w_in[:, :n_main].astype(BF16)
    w_kt = w_in[:, k_lo:k_lo + H_B * DQK_B].T.astype(BF16)
    w_gate = jnp.pad(w_in[:, n_main:], ((0, 0), (0, LANES - 4 * H_B))).astype(BF16)
    bg = jnp.pad(b_gate, (0, LANES - 4 * H_B)).reshape(1, LANES)
    tm = min(1024, S)
    p, kt, gates = _even_in(x2, g_pre.reshape(1, -1), w_main, w_kt, w_gate, bg, tm, 512)
    ya = _conv_module(p, conv_w, conv_b.reshape(1, -1), gn_g.reshape(1, -1), gn_b.reshape(1, -1),
                      B, S, min(256, S))
    gates_t = gates[:, :4 * H_B].reshape(B, S, 4 * H_B).transpose(0, 2, 1)
    hf, hb = _mlstm(p, kt, gates, gates_t, B, S, min(MLSTM_CHUNK, S))
    return _even_out(ya, hf, hb, p, x2, hn_g.reshape(1, -1), w_out.astype(BF16), g_post.reshape(1, -1),
                     min(512, S))


def _odd_layer(x2, B, S, cos, sin, g_pre, g_post, w_in, g_q, w_uq, g_kv, w_ukv, w_fd, w_out):
    T = B * S
    o = np.cumsum([0, Q_LORA, KV_LORA, QK_ROPE, W_D, W_C, W_D])
    seg = lambda i: w_in[:, o[i]:o[i + 1]]
    w_in_p = jnp.concatenate([seg(0), seg(1), seg(3), seg(4), seg(5), seg(2), seg(2)], axis=1).astype(BF16)
    wq = w_uq.reshape(Q_LORA, H_C, QK_NOPE + QK_ROPE)
    w_uq_p = jnp.concatenate([wq[:, :, :QK_NOPE].reshape(Q_LORA, -1),
                              wq[:, :, QK_NOPE:].reshape(Q_LORA, -1)], axis=1).astype(BF16)
    wkv = w_ukv.reshape(KV_LORA, H_C, QK_NOPE + V_HEAD)
    w_ukv_p = jnp.concatenate([wkv[:, :, :QK_NOPE].reshape(KV_LORA, -1),
                               wkv[:, :, QK_NOPE:].reshape(KV_LORA, -1)], axis=1).astype(BF16)
    cc = jnp.tile(cos.reshape(T, QK_ROPE // 2), (1, 4))
    ss = jnp.tile(jnp.concatenate([-sin, sin], axis=-1).reshape(T, QK_ROPE), (1, 2))
    tm = min(512, S)
    q, k, v, f4, zc, zd = _odd_in(x2, g_pre.reshape(1, -1), w_in_p, g_q.reshape(1, -1), w_uq_p,
                                  g_kv.reshape(1, -1), w_ukv_p, cc, ss, B, S, tm)
    att = _attention(q, k, v, B, S, min(512, S), min(512, S)).reshape(T, W_C)
    fr = _fft(f4, B, S).reshape(T, W_D)
    return _odd_out(att, zc, fr, zd, x2, w_fd.astype(BF16), w_out.astype(BF16), g_post.reshape(1, -1), tm)


def _rope_tables(positions):
    inv = ROPE_THETA ** (-jnp.arange(0, QK_ROPE, 2, dtype=F32) / QK_ROPE)
    ang = positions.astype(F32)[..., None] * inv
    return jnp.cos(ang), jnp.sin(ang)


def kernel(x, positions, even_g_pre, even_g_post, even_w_in, even_b_gate, even_conv_w, even_conv_b,
           even_gn_g, even_gn_b, even_hn_g, even_w_out, odd_g_pre, odd_g_post, odd_w_in, odd_g_q,
           odd_w_uq, odd_g_kv, odd_w_ukv, odd_w_fd, odd_w_out):
    B, S, _ = x.shape
    cos, sin = _rope_tables(positions)
    h = x.reshape(B * S, D_MODEL)
    depth = even_w_in.shape[0] + odd_w_in.shape[0]
    for layer in range(depth):
        j = layer // 2
        if layer % 2 == 0:
            h = _even_layer(h, B, S, even_g_pre[j], even_g_post[j], even_w_in[j], even_b_gate[j],
                            even_conv_w[j], even_conv_b[j], even_gn_g[j], even_gn_b[j], even_hn_g[j],
                            even_w_out[j])
        else:
            h = _odd_layer(h, B, S, cos, sin, odd_g_pre[j], odd_g_post[j], odd_w_in[j], odd_g_q[j],
                           odd_w_uq[j], odd_g_kv[j], odd_w_ukv[j], odd_w_fd[j], odd_w_out[j])
    return h.reshape(B, S, D_MODEL)
```

```python
import functools
import math

import numpy as np
import jax
import jax.numpy as jnp
from jax import lax
from jax.experimental import pallas as pl
from jax.experimental.pallas import tpu as pltpu

F32 = jnp.float32
BF16 = jnp.bfloat16

D_MODEL = 1024
RMS_EPS = 1e-6
LN_EPS = 1e-5
W_A = 1024
CONV_K = 31
A_GROUPS = 8
H_B = 4
DQK_B = 128
DV_B = 256
W_B = 1024
H_C = 8
QK_NOPE = 128
QK_ROPE = 64
V_HEAD = 128
Q_LORA = 384
KV_LORA = 256
W_C = 1024
ROPE_THETA = 10000.0
D_GROUPS = 4
D_GROUP_CH = 128
W_D = 512

LANES = 128
HALO = 16
QK_PAD = 256
VMEM_LIMIT = 56 * 1024 * 1024

MLSTM_CHUNK = 256
NEG_BIG = -1e30


def _params(sem, vmem=VMEM_LIMIT):
    return pltpu.CompilerParams(dimension_semantics=sem, vmem_limit_bytes=vmem)


def _sigmoid(x):
    return 1.0 / (1.0 + jnp.exp(-x))


def _silu(x):
    return x * _sigmoid(x)


def _log_sigmoid(x):
    return jnp.minimum(x, 0.0) - jnp.log(1.0 + jnp.exp(-jnp.abs(x)))


def _rms(x, g):
    return x * lax.rsqrt(jnp.mean(x * x, axis=-1, keepdims=True) + RMS_EPS) * g


def _dot(a, b):
    return jnp.dot(a, b, preferred_element_type=F32)


def _dot_nt(a, b):
    return lax.dot_general(a, b, (((1,), (1,)), ((), ())), preferred_element_type=F32)


def _dot_exact(a, b):
    return jnp.dot(a, b, preferred_element_type=F32, precision=lax.Precision.HIGHEST)


def _even_in_kernel(x_ref, g_ref, w_ref, wkt_ref, wg_ref, bg_ref, p_ref, kt_ref, gates_ref, h_scr):
    j = pl.program_id(1)

    @pl.when(j == 0)
    def _():
        h = _rms(x_ref[...], g_ref[...]).astype(BF16)
        h_scr[...] = h
        kt_ref[...] = _dot_nt(wkt_ref[...], h).astype(BF16)
        gates_ref[...] = _dot(h, wg_ref[...]) + bg_ref[...]

    p_ref[...] = _dot(h_scr[...], w_ref[...]).astype(BF16)


def _even_in(x2, g_pre, w_main, w_kt, w_gate, b_gate, tm, tn):
    T = x2.shape[0]
    n_main = w_main.shape[1]
    return pl.pallas_call(
        _even_in_kernel,
        grid=(T // tm, n_main // tn),
        in_specs=[
            pl.BlockSpec((tm, D_MODEL), lambda i, j: (i, 0)),
            pl.BlockSpec((1, D_MODEL), lambda i, j: (0, 0)),
            pl.BlockSpec((D_MODEL, tn), lambda i, j: (0, j)),
            pl.BlockSpec((H_B * DQK_B, D_MODEL), lambda i, j: (0, 0)),
            pl.BlockSpec((D_MODEL, LANES), lambda i, j: (0, 0)),
            pl.BlockSpec((1, LANES), lambda i, j: (0, 0)),
        ],
        out_specs=[
            pl.BlockSpec((tm, tn), lambda i, j: (i, j)),
            pl.BlockSpec((H_B * DQK_B, tm), lambda i, j: (0, i)),
            pl.BlockSpec((tm, LANES), lambda i, j: (i, 0)),
        ],
        out_shape=[
            jax.ShapeDtypeStruct((T, n_main), BF16),
            jax.ShapeDtypeStruct((H_B * DQK_B, T), BF16),
            jax.ShapeDtypeStruct((T, LANES), F32),
        ],
        scratch_shapes=[pltpu.VMEM((tm, D_MODEL), BF16)],
        compiler_params=_params(("arbitrary", "arbitrary")),
        name="even_in_proj",
    )(x2, g_pre, w_main, w_kt, w_gate, b_gate)


def _conv_kernel(av_ref, ag_ref, za_ref, avp_ref, agp_ref, avn_ref, agn_ref,
                 cw_ref, cb_ref, gg_ref, gb_ref, out_ref, u_scr, *, ts, rc):
    i = pl.program_id(1)
    last = pl.num_programs(1) - 1

    def gated(a_ref, g_ref):
        return a_ref[...].astype(F32) * _sigmoid(g_ref[...].astype(F32))

    u_scr[HALO:HALO + ts, :] = gated(av_ref, ag_ref)
    u_scr[0:HALO, :] = jnp.where(i > 0, gated(avp_ref, agp_ref), 0.0)
    u_scr[HALO + ts:HALO + ts + HALO, :] = jnp.where(i < last, gated(avn_ref, agn_ref), 0.0)

    first_tap = HALO - CONV_K // 2
    for g in range(A_GROUPS):
        cs = slice(g * LANES, (g + 1) * LANES)
        for r0 in range(0, ts, rc):
            acc = jnp.zeros((rc, LANES), F32) + cb_ref[:, cs]
            for j in range(CONV_K):
                lo = r0 + first_tap + j
                acc = acc + u_scr[lo:lo + rc, cs] * cw_ref[j:j + 1, cs]
            mu = jnp.mean(acc, axis=-1, keepdims=True)
            xc = acc - mu
            var = jnp.mean(xc * xc, axis=-1, keepdims=True)
            y = xc * lax.rsqrt(var + LN_EPS) * gg_ref[:, cs] + gb_ref[:, cs]
            y = _silu(y) * _silu(za_ref[r0:r0 + rc, cs].astype(F32))
            out_ref[r0:r0 + rc, cs] = y.astype(BF16)


def _conv_module(p, conv_w, conv_b, gn_g, gn_b, B, S, ts):
    T = B * S
    nt = S // ts
    hb = ts // HALO
    n_hblk = T // HALO

    def cur(col):
        return pl.BlockSpec((ts, W_A), lambda b, i: (b * nt + i, col))

    def prev(col):
        return pl.BlockSpec((HALO, W_A), lambda b, i: (jnp.maximum((b * nt + i) * hb - 1, 0), col))

    def nxt(col):
        return pl.BlockSpec((HALO, W_A), lambda b, i: (jnp.minimum((b * nt + i + 1) * hb, n_hblk - 1), col))

    def full(r):
        return pl.BlockSpec((r, W_A), lambda b, i: (0, 0))

    return pl.pallas_call(
        functools.partial(_conv_kernel, ts=ts, rc=min(ts, 128)),
        grid=(B, nt),
        in_specs=[cur(0), cur(1), cur(2), prev(0), prev(1), nxt(0), nxt(1),
                  full(CONV_K), full(1), full(1), full(1)],
        out_specs=pl.BlockSpec((ts, W_A), lambda b, i: (b * nt + i, 0)),
        out_shape=jax.ShapeDtypeStruct((T, W_A), BF16),
        scratch_shapes=[pltpu.VMEM((ts + 2 * HALO, W_A), F32)],
        compiler_params=_params(("arbitrary", "arbitrary")),
        name="conv_module",
    )(p, p, p, p, p, p, p, conv_w, conv_b, gn_g, gn_b)


def _mlstm_kernel(qf_ref, kf_ref, ktf_ref, vf_ref, gcf_ref, grf_ref,
                  qb_ref, kb_ref, ktb_ref, vb_ref, gcb_ref, grb_ref,
                  hf_ref, hb_ref, c_scr, n_scr, m_scr, *, L):
    c = pl.program_id(1)

    @pl.when(c == 0)
    def _():
        c_scr[...] = jnp.zeros_like(c_scr)
        n_scr[...] = jnp.zeros_like(n_scr)
        m_scr[...] = jnp.zeros_like(m_scr)

    row = lax.broadcasted_iota(jnp.int32, (L, L), 0)
    col = lax.broadcasted_iota(jnp.int32, (L, L), 1)
    lower = col <= row
    upper = col >= row
    lower_f = lower.astype(F32)
    upper_f = upper.astype(F32)
    scale = DQK_B ** -0.5

    def direction(q_ref, k_ref, kt_ref, v_ref, gc_ref, gr_ref, h_ref, fwd):
        gate_i = 0 if fwd else 2 * H_B
        gate_f = gate_i + H_B
        mask = lower if fwd else upper
        ls_col = _log_sigmoid(gc_ref[...])
        ls_row = _log_sigmoid(gr_ref[...])
        b_col_all = _dot_exact(lower_f if fwd else upper_f, ls_col)
        b_row_all = _dot_exact(ls_row, upper_f if fwd else lower_f)
        gc = gc_ref[...]
        gr = gr_ref[...]
        for h in range(H_B):
            sidx = h if fwd else H_B + h
            b_col = b_col_all[:, gate_f + h:gate_f + h + 1]
            b_row = b_row_all[gate_f + h:gate_f + h + 1, :]
            i_col = gc[:, gate_i + h:gate_i + h + 1]
            i_row = gr[gate_i + h:gate_i + h + 1, :]
            q = (q_ref[:, h * DQK_B:(h + 1) * DQK_B].astype(F32) * scale).astype(BF16)
            k = k_ref[:, h * DQK_B:(h + 1) * DQK_B]
            kt = kt_ref[h * DQK_B:(h + 1) * DQK_B, :]
            v = v_ref[:, h * DV_B:(h + 1) * DV_B]
            m = m_scr[sidx][:, 0:1]
            cst = c_scr[sidx]
            nst = n_scr[sidx]

            dmat = jnp.where(mask, b_col - b_row + i_row, -jnp.inf)
            a = b_col + m
            m_t = jnp.maximum(a, jnp.max(dmat, axis=1, keepdims=True))
            dw = jnp.exp(dmat - m_t)
            aw = jnp.exp(a - m_t)
            s = _dot(q, kt) * dw
            num = _dot(s.astype(BF16), v) + aw * _dot(q, cst.astype(BF16))
            den = jnp.sum(s, axis=1, keepdims=True) + aw * jnp.sum(q.astype(F32) * nst, axis=1, keepdims=True)
            hval = num / jnp.maximum(jnp.abs(den), jnp.exp(-m_t))
            h_ref[:, h * DV_B:(h + 1) * DV_B] = hval.astype(BF16)

            b_end = b_col[L - 1:L, :] if fwd else b_col[0:1, :]
            gl = b_end - b_col + i_col
            m_new = jnp.maximum(b_end + m, jnp.max(gl, axis=0, keepdims=True))
            wk = jnp.exp(gl - m_new)
            decay = jnp.exp(b_end + m - m_new)
            c_scr[sidx] = decay * cst + _dot(kt, (wk * v.astype(F32)).astype(BF16))
            n_scr[sidx] = decay * nst + jnp.sum(wk * k.astype(F32), axis=0, keepdims=True)
            m_scr[sidx] = jnp.broadcast_to(m_new, (1, LANES))

    direction(qf_ref, kf_ref, ktf_ref, vf_ref, gcf_ref, grf_ref, hf_ref, True)
    direction(qb_ref, kb_ref, ktb_ref, vb_ref, gcb_ref, grb_ref, hb_ref, False)


def _mlstm(p, kt, gates, gates_t, B, S, L):
    T = B * S
    nc = S // L
    qcol = 3 * W_A // (H_B * DQK_B)
    vcol = (3 * W_A + 2 * H_B * DQK_B) // W_B

    def specs(chunk):
        return [
            pl.BlockSpec((L, H_B * DQK_B), lambda b, c: (b * nc + chunk(c), qcol)),
            pl.BlockSpec((L, H_B * DQK_B), lambda b, c: (b * nc + chunk(c), qcol + 1)),
            pl.BlockSpec((H_B * DQK_B, L), lambda b, c: (0, b * nc + chunk(c))),
            pl.BlockSpec((L, W_B), lambda b, c: (b * nc + chunk(c), vcol)),
            pl.BlockSpec((L, LANES), lambda b, c: (b * nc + chunk(c), 0)),
            pl.BlockSpec((None, 4 * H_B, L), lambda b, c: (b, 0, chunk(c))),
        ]

    fwd = lambda c: c
    bwd = lambda c: nc - 1 - c
    n_streams = 2 * H_B
    return pl.pallas_call(
        functools.partial(_mlstm_kernel, L=L),
        grid=(B, nc),
        in_specs=specs(fwd) + specs(bwd),
        out_specs=[
            pl.BlockSpec((L, W_B), lambda b, c: (b * nc + c, 0)),
            pl.BlockSpec((L, W_B), lambda b, c: (b * nc + nc - 1 - c, 0)),
        ],
        out_shape=[jax.ShapeDtypeStruct((T, W_B), BF16)] * 2,
        scratch_shapes=[
            pltpu.VMEM((n_streams, DQK_B, DV_B), F32),
            pltpu.VMEM((n_streams, 1, DQK_B), F32),
            pltpu.VMEM((n_streams, 1, LANES), F32),
        ],
        compiler_params=_params(("arbitrary", "arbitrary")),
        name="mlstm",
    )(p, p, kt, p, gates, gates_t, p, p, kt, p, gates, gates_t)


def _even_out_kernel(ya_ref, hf_ref, hb_ref, o_ref, zb_ref, x_ref, hn_ref, w_ref, g_ref, out_ref):
    parts = []
    for h in range(H_B):
        cs = slice(h * DV_B, (h + 1) * DV_B)
        hm = hf_ref[:, cs].astype(F32) + hb_ref[:, cs].astype(F32)
        hm = _rms(hm, hn_ref[:, cs])
        yb = _sigmoid(o_ref[:, cs].astype(F32)) * hm * _silu(zb_ref[:, cs].astype(F32))
        parts.append(yb.astype(BF16))
    y = _dot(ya_ref[...], w_ref[0:W_A, :])
    for h in range(H_B):
        y = y + _dot(parts[h], w_ref[W_A + h * DV_B:W_A + (h + 1) * DV_B, :])
    out_ref[...] = x_ref[...] + _rms(y, g_ref[...])


def _even_out(ya, hf, hb, p, x2, hn_g, w_out, g_post, tm):
    T = x2.shape[0]
    ocol = (3 * W_A + 2 * H_B * DQK_B + W_B) // W_B
    row = lambda w: pl.BlockSpec((tm, w), lambda i: (i, 0))
    return pl.pallas_call(
        _even_out_kernel,
        grid=(T // tm,),
        in_specs=[row(W_A), row(W_B), row(W_B),
                  pl.BlockSpec((tm, W_B), lambda i: (i, ocol)),
                  pl.BlockSpec((tm, W_B), lambda i: (i, ocol + 1)),
                  row(D_MODEL),
                  pl.BlockSpec((1, W_B), lambda i: (0, 0)),
                  pl.BlockSpec((W_A + W_B, D_MODEL), lambda i: (0, 0)),
                  pl.BlockSpec((1, D_MODEL), lambda i: (0, 0))],
        out_specs=row(D_MODEL),
        out_shape=jax.ShapeDtypeStruct((T, D_MODEL), F32),
        compiler_params=_params(("arbitrary",)),
        name="even_out_proj",
    )(ya, hf, hb, p, p, x2, hn_g, w_out, g_post)


def _rope128(x, cc, ss):
    lane = lax.broadcasted_iota(jnp.int32, x.shape, 1)
    swapped = jnp.where(lane % QK_ROPE < QK_ROPE // 2,
                        pltpu.roll(x, LANES - QK_ROPE // 2, 1),
                        pltpu.roll(x, QK_ROPE // 2, 1))
    return x * cc + swapped * ss


def _odd_in_kernel(x_ref, g_ref, w_ref, gq_ref, wuqt_ref, gkv_ref, wuk_ref, wuvt_ref,
                   cc_ref, ss_ref, cost_ref, sint_ref,
                   qt_ref, k_ref, vt_ref, f_ref, zc_ref, zd_ref):
    tm = x_ref.shape[0]
    h = _rms(x_ref[...], g_ref[...]).astype(BF16)
    o_ckv = Q_LORA
    o_f = o_ckv + KV_LORA
    o_zc = o_f + W_D
    o_zd = o_zc + W_C
    o_kr = o_zd + W_D
    half = QK_ROPE // 2

    for g in range(D_GROUPS):
        f_ref[g] = _dot(h, w_ref[:, o_f + g * D_GROUP_CH:o_f + (g + 1) * D_GROUP_CH]).astype(BF16)
    zc_ref[...] = _dot(h, w_ref[:, o_zc:o_zc + W_C]).astype(BF16)
    zd_ref[...] = _dot(h, w_ref[:, o_zd:o_zd + W_D]).astype(BF16)

    ckv = _rms(_dot(h, w_ref[:, o_ckv:o_ckv + KV_LORA]), gkv_ref[...]).astype(BF16)
    lane = lax.broadcasted_iota(jnp.int32, (tm, LANES), 1)
    first_half = lane < QK_ROPE
    kr = _rope128(_dot(h, w_ref[:, o_kr:o_kr + LANES]), cc_ref[...], ss_ref[...])
    kr_even = jnp.where(first_half, kr, 0.0).astype(BF16)
    kr_odd = jnp.where(first_half, 0.0, kr).astype(BF16)
    for hd in range(H_C):
        k_ref[hd, :, 0:QK_NOPE] = _dot(ckv, wuk_ref[:, hd * QK_NOPE:(hd + 1) * QK_NOPE]).astype(BF16)
        k_ref[hd, :, QK_NOPE:QK_PAD] = kr_even if hd % 2 == 0 else kr_odd

    vt = _dot_nt(wuvt_ref[...], ckv)
    for hd in range(H_C):
        vt_ref[hd] = vt[hd * V_HEAD:(hd + 1) * V_HEAD, :].astype(BF16)

    scale = (QK_NOPE + QK_ROPE) ** -0.5 * math.log2(math.e)
    cq = _rms(_dot(h, w_ref[:, 0:Q_LORA]), gq_ref[...]).astype(BF16)
    qn = _dot_nt(wuqt_ref[0:H_C * QK_NOPE, :], cq) * scale
    qr = _dot_nt(wuqt_ref[H_C * QK_NOPE:H_C * (QK_NOPE + QK_ROPE), :], cq) * scale
    ct = cost_ref[...]
    st = sint_ref[...]
    zeros = jnp.zeros((QK_ROPE, tm), BF16)
    for hd in range(H_C):
        qt_ref[hd, 0:QK_NOPE, :] = qn[hd * QK_NOPE:(hd + 1) * QK_NOPE, :].astype(BF16)
        x1 = qr[hd * QK_ROPE:hd * QK_ROPE + half, :]
        x2 = qr[hd * QK_ROPE + half:(hd + 1) * QK_ROPE, :]
        lo = QK_NOPE if hd % 2 == 0 else QK_NOPE + QK_ROPE
        pad = QK_NOPE + QK_ROPE if hd % 2 == 0 else QK_NOPE
        qt_ref[hd, lo:lo + half, :] = (x1 * ct - x2 * st).astype(BF16)
        qt_ref[hd, lo + half:lo + QK_ROPE, :] = (x1 * st + x2 * ct).astype(BF16)
        qt_ref[hd, pad:pad + QK_ROPE, :] = zeros


def _odd_in(x2, g_pre, w_in, g_q, w_uq_t, g_kv, w_uk, w_uv_t, cc, ss, cos_t, sin_t, B, S, tm):
    T = B * S
    nt = S // tm
    full = lambda a: pl.BlockSpec(a.shape, lambda i: (0,) * a.ndim)
    row = lambda w: pl.BlockSpec((tm, w), lambda i: (i, 0))
    return pl.pallas_call(
        _odd_in_kernel,
        grid=(T // tm,),
        in_specs=[row(D_MODEL), full(g_pre), full(w_in), full(g_q), full(w_uq_t), full(g_kv), full(w_uk),
                  full(w_uv_t), row(LANES), row(LANES),
                  pl.BlockSpec((QK_ROPE // 2, tm), lambda i: (0, i)),
                  pl.BlockSpec((QK_ROPE // 2, tm), lambda i: (0, i))],
        out_specs=[pl.BlockSpec((None, H_C, QK_PAD, tm), lambda i: (i // nt, 0, 0, i % nt)),
                   pl.BlockSpec((None, H_C, tm, QK_PAD), lambda i: (i // nt, 0, i % nt, 0)),
                   pl.BlockSpec((None, H_C, None, V_HEAD, tm), lambda i: (i // nt, 0, i % nt, 0, 0)),
                   pl.BlockSpec((D_GROUPS, None, tm, D_GROUP_CH), lambda i: (0, i // nt, i % nt, 0)),
                   row(W_C), row(W_D)],
        out_shape=[
            jax.ShapeDtypeStruct((B, H_C, QK_PAD, S), BF16),
            jax.ShapeDtypeStruct((B, H_C, S, QK_PAD), BF16),
            jax.ShapeDtypeStruct((B, H_C, nt, V_HEAD, tm), BF16),
            jax.ShapeDtypeStruct((D_GROUPS, B, S, D_GROUP_CH), BF16),
            jax.ShapeDtypeStruct((T, W_C), BF16),
            jax.ShapeDtypeStruct((T, W_D), BF16),
        ],
        compiler_params=_params(("arbitrary",)),
        name="odd_in_proj",
    )(x2, g_pre, w_in, g_q, w_uq_t, g_kv, w_uk, w_uv_t, cc, ss, cos_t, sin_t)


def _attn_kernel(qt_ref, k_ref, vt_ref, o_ref, s_a, s_b, p_a, p_b, acc_scr, *, tk):
    nk = k_ref.shape[0] // tk
    qt = qt_ref[...]
    tq = qt.shape[1]

    def scores(j):
        return _dot(k_ref[pl.ds(pl.multiple_of(j * tk, tk), tk), :], qt)

    def half_step(j, s_cur, s_next, p_prev, p_cur, carry, with_next):
        m, l, alpha_prev = carry
        if with_next:
            s_next[...] = scores(j + 1)
        s = s_cur[...]
        m_new = jnp.maximum(m, jnp.max(s, axis=0, keepdims=True))
        alpha = jnp.exp2(m - m_new)
        p = jnp.exp2(s - m_new)
        l = alpha * l + jnp.sum(p, axis=0, keepdims=True)
        acc_scr[...] = alpha_prev * acc_scr[...] + _dot(vt_ref[jnp.maximum(j - 1, 0)], p_prev[...])
        p_cur[...] = p.astype(BF16)
        return m_new, l, alpha

    s_a[...] = scores(0)
    p_b[...] = jnp.zeros_like(p_b)
    acc_scr[...] = jnp.zeros_like(acc_scr)

    def pair(i, carry):
        j = 2 * i
        carry = half_step(j, s_a, s_b, p_b, p_a, carry, True)
        return half_step(j + 1, s_b, s_a, p_a, p_b, carry, True)

    init = (jnp.full((1, tq), NEG_BIG, F32), jnp.zeros((1, tq), F32), jnp.ones((1, tq), F32))
    carry = lax.fori_loop(0, nk // 2 - 1, pair, init)
    carry = half_step(nk - 2, s_a, s_b, p_b, p_a, carry, True)
    m, l, alpha_prev = half_step(nk - 1, s_b, s_a, p_a, p_b, carry, False)
    acc = alpha_prev * acc_scr[...] + _dot(vt_ref[nk - 1], p_b[...])
    o_ref[...] = (acc / l).T.astype(BF16)


def _attention(qt, k, vt, B, S, tq, tk):
    nk = S // tk
    assert nk >= 2 and nk % 2 == 0 and vt.shape[2] == nk
    return pl.pallas_call(
        functools.partial(_attn_kernel, tk=tk),
        grid=(B, H_C, S // tq),
        in_specs=[
            pl.BlockSpec((None, None, QK_PAD, tq), lambda b, h, i: (b, h, 0, i)),
            pl.BlockSpec((None, None, S, QK_PAD), lambda b, h, i: (b, h, 0, 0)),
            pl.BlockSpec((None, None, nk, V_HEAD, tk), lambda b, h, i: (b, h, 0, 0, 0)),
        ],
        out_specs=pl.BlockSpec((None, tq, V_HEAD), lambda b, h, i: (b, i, h)),
        out_shape=jax.ShapeDtypeStruct((B, S, W_C), BF16),
        scratch_shapes=[pltpu.VMEM((tk, tq), F32), pltpu.VMEM((tk, tq), F32),
                        pltpu.VMEM((tk, tq), BF16), pltpu.VMEM((tk, tq), BF16),
                        pltpu.VMEM((V_HEAD, tq), F32)],
        compiler_params=_params(("arbitrary", "arbitrary", "arbitrary")),
        name="flash_attention",
    )(qt, k, vt)


def _fft_tables(S):
    n1 = S // LANES
    a = np.arange(n1, dtype=np.float64)
    ang1 = 2.0 * np.pi * np.outer(a, a) / n1
    w1 = np.concatenate([np.cos(ang1), -np.sin(ang1)], axis=0)
    k1 = np.arange(n1)[:, None, None]
    k2 = np.arange(LANES)[None, :, None]
    n2 = np.arange(LANES)[None, None, :]
    kk = (k1 + n1 * k2) * n2 % S
    ang2 = 2.0 * np.pi * kk.astype(np.float64) / S
    gc = np.cos(ang2).reshape(n1 * LANES, LANES)
    gs = np.sin(ang2).reshape(n1 * LANES, LANES)
    c = np.arange(D_GROUP_CH, dtype=np.float64)
    angc = 2.0 * np.pi * np.outer(c, c) / D_GROUP_CH
    f32 = lambda t: jnp.asarray(t.astype(np.float32))
    return f32(w1), f32(gc), f32(gs), f32(np.cos(angc)), f32(np.sin(angc))


def _fft_kernel(x_ref, w1_ref, gc_ref, gs_ref, cc_ref, sc_ref, out_ref, ar_scr, ai_scr, *, n1, cw):
    n_chunks = (LANES * LANES) // cw
    per = cw // LANES
    w1 = w1_ref[...]

    def stage1(ch, carry):
        lo = pl.multiple_of(ch * cw, cw)
        res = _dot(w1, x_ref[:, pl.ds(lo, cw)])
        for j in range(per):
            r0 = pl.multiple_of((ch * per + j) * n1, n1)
            ar_scr[pl.ds(r0, n1), :] = res[0:n1, j * LANES:(j + 1) * LANES]
            ai_scr[pl.ds(r0, n1), :] = res[n1:2 * n1, j * LANES:(j + 1) * LANES]
        return carry

    lax.fori_loop(0, n_chunks, stage1, 0)

    ccm = cc_ref[...]
    scm = sc_ref[...]
    norm = 1.0 / math.sqrt(n1 * LANES * D_GROUP_CH)

    def stage2(k1, carry):
        ar = ar_scr[pl.ds(k1, LANES, stride=n1), :]
        ai = ai_scr[pl.ds(k1, LANES, stride=n1), :]
        a = jnp.concatenate([ar, ai], axis=1).astype(BF16)
        t0 = pl.multiple_of(k1 * LANES, LANES)
        p1 = _dot(gc_ref[pl.ds(t0, LANES), :], a)
        p2 = _dot(gs_ref[pl.ds(t0, LANES), :], a)
        zr = p1[:, 0:LANES] + p2[:, LANES:2 * LANES]
        zi = p1[:, LANES:2 * LANES] - p2[:, 0:LANES]
        fr = _dot(zr.astype(BF16), ccm) + _dot(zi.astype(BF16), scm)
        out_ref[pl.ds(k1, LANES, stride=n1), :] = fr * norm
        return carry

    lax.fori_loop(0, n1, stage2, 0)


def _fft(f4, B, S):
    n1 = S // LANES
    w1, gc, gs, cc, sc = _fft_tables(S)
    bf = lambda t: t.astype(BF16)
    x4 = f4.reshape(D_GROUPS, B, n1, LANES * D_GROUP_CH)
    cw = 2048
    full = lambda a: pl.BlockSpec(a.shape, lambda g, b: (0,) * a.ndim)
    tabs = [bf(w1), bf(gc), bf(gs), bf(cc), bf(sc)]
    return pl.pallas_call(
        functools.partial(_fft_kernel, n1=n1, cw=cw),
        grid=(D_GROUPS, B),
        in_specs=[pl.BlockSpec((None, None, n1, LANES * D_GROUP_CH), lambda g, b: (g, b, 0, 0))]
                 + [full(t) for t in tabs],
        out_specs=pl.BlockSpec((None, S, D_GROUP_CH), lambda g, b: (b, 0, g)),
        out_shape=jax.ShapeDtypeStruct((B, S, W_D), F32),
        scratch_shapes=[pltpu.VMEM((S, D_GROUP_CH), F32), pltpu.VMEM((S, D_GROUP_CH), F32)],
        compiler_params=_params(("arbitrary", "arbitrary")),
        name="fft2_real",
    )(x4, *tabs)


def _odd_out_kernel(att_ref, zc_ref, fr_ref, zd_ref, x_ref, wfd_ref, w_ref, g_ref, out_ref):
    yc = (att_ref[...].astype(F32) * _silu(zc_ref[...].astype(F32))).astype(BF16)
    yd = (_dot(fr_ref[...].astype(BF16), wfd_ref[...]) * _silu(zd_ref[...].astype(F32))).astype(BF16)
    y = _dot(yc, w_ref[0:W_C, :]) + _dot(yd, w_ref[W_C:W_C + W_D, :])
    out_ref[...] = x_ref[...] + _rms(y, g_ref[...])


def _odd_out(att, zc, fr, zd, x2, w_fd, w_out, g_post, tm):
    T = x2.shape[0]
    row = lambda w: pl.BlockSpec((tm, w), lambda i: (i, 0))
    full = lambda a: pl.BlockSpec(a.shape, lambda i: (0,) * a.ndim)
    return pl.pallas_call(
        _odd_out_kernel,
        grid=(T // tm,),
        in_specs=[row(W_C), row(W_C), row(W_D), row(W_D), row(D_MODEL), full(w_fd), full(w_out), full(g_post)],
        out_specs=row(D_MODEL),
        out_shape=jax.ShapeDtypeStruct((T, D_MODEL), F32),
        compiler_params=_params(("arbitrary",)),
        name="odd_out_proj",
    )(att, zc, fr, zd, x2, w_fd, w_out, g_post)


def _even_layer(x2, B, S, g_pre, g_post, w_in, b_gate, conv_w, conv_b, gn_g, gn_b, hn_g, w_out):
    n_main = 3 * W_A + 2 * H_B * DQK_B + 3 * W_B
    k_lo = 3 * W_A + H_B * DQK_B
    w_main = w_in[:, :n_main].astype(BF16)
    w_kt = w_in[:, k_lo:k_lo + H_B * DQK_B].T.astype(BF16)
    w_gate = jnp.pad(w_in[:, n_main:], ((0, 0), (0, LANES - 4 * H_B))).astype(BF16)
    bg = jnp.pad(b_gate, (0, LANES - 4 * H_B)).reshape(1, LANES)
    tm = min(1024, S)
    p, kt, gates = _even_in(x2, g_pre.reshape(1, -1), w_main, w_kt, w_gate, bg, tm, 512)
    ya = _conv_module(p, conv_w, conv_b.reshape(1, -1), gn_g.reshape(1, -1), gn_b.reshape(1, -1),
                      B, S, min(256, S))
    gates_t = gates[:, :4 * H_B].reshape(B, S, 4 * H_B).transpose(0, 2, 1)
    hf, hb = _mlstm(p, kt, gates, gates_t, B, S, min(MLSTM_CHUNK, S))
    return _even_out(ya, hf, hb, p, x2, hn_g.reshape(1, -1), w_out.astype(BF16), g_post.reshape(1, -1),
                     min(512, S))


def _odd_layer(x2, B, S, cos, sin, g_pre, g_post, w_in, g_q, w_uq, g_kv, w_ukv, w_fd, w_out):
    T = B * S
    o = np.cumsum([0, Q_LORA, KV_LORA, QK_ROPE, W_D, W_C, W_D])
    seg = lambda i: w_in[:, o[i]:o[i + 1]]
    w_in_p = jnp.concatenate([seg(0), seg(1), seg(3), seg(4), seg(5), seg(2), seg(2)], axis=1).astype(BF16)
    wq = w_uq.reshape(Q_LORA, H_C, QK_NOPE + QK_ROPE)
    w_uq_t = jnp.concatenate([wq[:, :, :QK_NOPE].reshape(Q_LORA, -1),
                              wq[:, :, QK_NOPE:].reshape(Q_LORA, -1)], axis=1).T.astype(BF16)
    wkv = w_ukv.reshape(KV_LORA, H_C, QK_NOPE + V_HEAD)
    w_uk = wkv[:, :, :QK_NOPE].reshape(KV_LORA, -1).astype(BF16)
    w_uv_t = wkv[:, :, QK_NOPE:].reshape(KV_LORA, -1).T.astype(BF16)
    cos2 = cos.reshape(T, QK_ROPE // 2)
    sin2 = sin.reshape(T, QK_ROPE // 2)
    cc = jnp.tile(cos2, (1, 4))
    ss = jnp.tile(jnp.concatenate([-sin2, sin2], axis=-1), (1, 2))
    tm = min(512, S // 2)
    qt, k, vt, f4, zc, zd = _odd_in(x2, g_pre.reshape(1, -1), w_in_p, g_q.reshape(1, -1), w_uq_t,
                                    g_kv.reshape(1, -1), w_uk, w_uv_t, cc, ss, cos2.T, sin2.T, B, S, tm)
    att = _attention(qt, k, vt, B, S, min(512, S), tm).reshape(T, W_C)
    fr = _fft(f4, B, S).reshape(T, W_D)
    return _odd_out(att, zc, fr, zd, x2, w_fd.astype(BF16), w_out.astype(BF16), g_post.reshape(1, -1), tm)


def _rope_tables(positions):
    inv = ROPE_THETA ** (-jnp.arange(0, QK_ROPE, 2, dtype=F32) / QK_ROPE)
    ang = positions.astype(F32)[..., None] * inv
    return jnp.cos(ang), jnp.sin(ang)


def kernel(x, positions, even_g_pre, even_g_post, even_w_in, even_b_gate, even_conv_w, even_conv_b,
           even_gn_g, even_gn_b, even_hn_g, even_w_out, odd_g_pre, odd_g_post, odd_w_in, odd_g_q,
           odd_w_uq, odd_g_kv, odd_w_ukv, odd_w_fd, odd_w_out):
    B, S, _ = x.shape
    cos, sin = _rope_tables(positions)
    h = x.reshape(B * S, D_MODEL)
    depth = even_w_in.shape[0] + odd_w_in.shape[0]
    for layer in range(depth):
        j = layer // 2
        if layer % 2 == 0:
            h = _even_layer(h, B, S, even_g_pre[j], even_g_post[j], even_w_in[j], even_b_gate[j],
                            even_conv_w[j], even_conv_b[j], even_gn_g[j], even_gn_b[j], even_hn_g[j],
                            even_w_out[j])
        else:
            h = _odd_layer(h, B, S, cos, sin, odd_g_pre[j], odd_g_post[j], odd_w_in[j], odd_g_q[j],
                           odd_w_uq[j], odd_g_kv[j], odd_w_ukv[j], odd_w_fd[j], odd_w_out[j])
    return h.reshape(B, S, D_MODEL)
```

```python
import functools
import math

import numpy as np
import jax
import jax.numpy as jnp
from jax import lax
from jax.experimental import pallas as pl
from jax.experimental.pallas import tpu as pltpu

F32 = jnp.float32
BF16 = jnp.bfloat16

D_MODEL = 1024
RMS_EPS = 1e-6
LN_EPS = 1e-5
W_A = 1024
CONV_K = 31
A_GROUPS = 8
H_B = 4
DQK_B = 128
DV_B = 256
W_B = 1024
H_C = 8
QK_NOPE = 128
QK_ROPE = 64
V_HEAD = 128
Q_LORA = 384
KV_LORA = 256
W_C = 1024
ROPE_THETA = 10000.0
D_GROUPS = 4
D_GROUP_CH = 128
W_D = 512

LANES = 128
HALO = 16
QK_PAD = 256
V_AUG = V_HEAD + 16
VMEM_LIMIT = 56 * 1024 * 1024

MLSTM_CHUNK = 256
MLSTM_AUG = DV_B + 16
EVEN_O_OFFSET = 3 * W_A
EVEN_K_OFFSET = 3 * W_A + 2 * W_B
EVEN_MAIN = EVEN_K_OFFSET + H_B * DQK_B
ATTN_TQ = 512
ATTN_TK = 512
NEG_BIG = -1e30


def _params(sem, vmem=VMEM_LIMIT):
    return pltpu.CompilerParams(dimension_semantics=sem, vmem_limit_bytes=vmem)


def _sigmoid(x):
    return 1.0 / (1.0 + jnp.exp(-x))


def _silu(x):
    return x * _sigmoid(x)


def _log_sigmoid(x):
    return jnp.minimum(x, 0.0) - jnp.log(1.0 + jnp.exp(-jnp.abs(x)))


def _rms(x, g):
    return x * lax.rsqrt(jnp.mean(x * x, axis=-1, keepdims=True) + RMS_EPS) * g


def _dot(a, b):
    return jnp.dot(a, b, preferred_element_type=F32)


def _dot_nt(a, b):
    return lax.dot_general(a, b, (((1,), (1,)), ((), ())), preferred_element_type=F32)


def _split3(x):
    hi = x.astype(BF16)
    r1 = x - hi.astype(F32)
    mid = r1.astype(BF16)
    lo = (r1 - mid.astype(F32)).astype(BF16)
    return hi, mid, lo


def _tri_sum_left(tri, x):
    hi, mid, lo = _split3(x)
    return _dot(tri, lo) + _dot(tri, mid) + _dot(tri, hi)


def _tri_sum_right(x, tri):
    hi, mid, lo = _split3(x)
    return _dot(lo, tri) + _dot(mid, tri) + _dot(hi, tri)


def _even_in_kernel(x_ref, g_ref, w_ref, wqt_ref, wvt_ref, wg_ref, bg_ref,
                    p_ref, qt_ref, vt_ref, gates_ref, h_scr):
    j = pl.program_id(1)

    @pl.when(j == 0)
    def _():
        h = _rms(x_ref[...], g_ref[...]).astype(BF16)
        h_scr[...] = h
        qt_ref[...] = _dot_nt(wqt_ref[...], h).astype(BF16)
        vt_ref[...] = _dot_nt(wvt_ref[...], h).astype(BF16)
        gates_ref[...] = _dot(h, wg_ref[...]) + bg_ref[...]

    p_ref[...] = _dot(h_scr[...], w_ref[...]).astype(BF16)


def _even_in(x2, g_pre, w_main, w_qt, w_vt, w_gate, b_gate, tm, tn):
    T = x2.shape[0]
    n_main = w_main.shape[1]
    full = lambda a: pl.BlockSpec(a.shape, lambda i, j: (0, 0))
    return pl.pallas_call(
        _even_in_kernel,
        grid=(T // tm, n_main // tn),
        in_specs=[
            pl.BlockSpec((tm, D_MODEL), lambda i, j: (i, 0)),
            full(g_pre),
            pl.BlockSpec((D_MODEL, tn), lambda i, j: (0, j)),
            full(w_qt), full(w_vt), full(w_gate), full(b_gate),
        ],
        out_specs=[
            pl.BlockSpec((tm, tn), lambda i, j: (i, j)),
            pl.BlockSpec((H_B * DQK_B, tm), lambda i, j: (0, i)),
            pl.BlockSpec((W_B, tm), lambda i, j: (0, i)),
            pl.BlockSpec((tm, LANES), lambda i, j: (i, 0)),
        ],
        out_shape=[
            jax.ShapeDtypeStruct((T, n_main), BF16),
            jax.ShapeDtypeStruct((H_B * DQK_B, T), BF16),
            jax.ShapeDtypeStruct((W_B, T), BF16),
            jax.ShapeDtypeStruct((T, LANES), F32),
        ],
        scratch_shapes=[pltpu.VMEM((tm, D_MODEL), BF16)],
        compiler_params=_params(("arbitrary", "arbitrary")),
        name="even_in_proj",
    )(x2, g_pre, w_main, w_qt, w_vt, w_gate, b_gate)


CONV_FIRST_TAP = HALO - CONV_K // 2
CONV_SUB = 8
CONV_TILE_TAPS = (CONV_FIRST_TAP + CONV_K - 1) // CONV_SUB + 1


def _conv_shift_matrix(rc):
    span = rc + (CONV_TILE_TAPS - 1) * CONV_SUB
    win = rc + 2 * HALO
    m = np.zeros((CONV_SUB * span, win), np.float32)
    for b in range(CONV_SUB):
        m[b * span + np.arange(span), np.arange(span) + b] = 1.0
    return jnp.asarray(m).astype(BF16)


def _conv_kernel(av_ref, ag_ref, za_ref, avp_ref, agp_ref, avn_ref, agn_ref,
                 cw_ref, cb_ref, gg_ref, gb_ref, sm_ref, out_ref, u_scr, sh_scr, *, ts, rc):
    i = pl.program_id(1)
    last = pl.num_programs(1) - 1

    def gated(a_ref, g_ref):
        return a_ref[...].astype(F32) * _sigmoid(g_ref[...].astype(F32))

    u_scr[HALO:HALO + ts, :] = gated(av_ref, ag_ref).astype(BF16)
    u_scr[0:HALO, :] = jnp.where(i > 0, gated(avp_ref, agp_ref), 0.0).astype(BF16)
    u_scr[HALO + ts:HALO + ts + HALO, :] = jnp.where(i < last, gated(avn_ref, agn_ref), 0.0).astype(BF16)

    span = rc + (CONV_TILE_TAPS - 1) * CONV_SUB
    for r0 in range(0, ts, rc):
        sh_scr[...] = _dot(sm_ref[...], u_scr[r0:r0 + rc + 2 * HALO, :])
        for g in range(A_GROUPS):
            cs = slice(g * LANES, (g + 1) * LANES)
            acc = jnp.zeros((rc, LANES), F32) + cb_ref[:, cs]
            for b in range(CONV_SUB):
                for a in range(CONV_TILE_TAPS):
                    j = CONV_SUB * a + b - CONV_FIRST_TAP
                    if 0 <= j < CONV_K:
                        lo = b * span + CONV_SUB * a
                        acc = acc + sh_scr[lo:lo + rc, cs] * cw_ref[j:j + 1, cs]
            mu = jnp.mean(acc, axis=-1, keepdims=True)
            xc = acc - mu
            var = jnp.mean(xc * xc, axis=-1, keepdims=True)
            y = xc * lax.rsqrt(var + LN_EPS) * gg_ref[:, cs] + gb_ref[:, cs]
            y = _silu(y) * _silu(za_ref[r0:r0 + rc, cs].astype(F32))
            out_ref[r0:r0 + rc, cs] = y.astype(BF16)


def _conv_module(p, conv_w, conv_b, gn_g, gn_b, B, S, ts):
    T = B * S
    nt = S // ts
    hb = ts // HALO
    n_hblk = T // HALO

    def cur(col):
        return pl.BlockSpec((ts, W_A), lambda b, i: (b * nt + i, col))

    def prev(col):
        return pl.BlockSpec((HALO, W_A), lambda b, i: (jnp.maximum((b * nt + i) * hb - 1, 0), col))

    def nxt(col):
        return pl.BlockSpec((HALO, W_A), lambda b, i: (jnp.minimum((b * nt + i + 1) * hb, n_hblk - 1), col))

    def full(r):
        return pl.BlockSpec((r, W_A), lambda b, i: (0, 0))

    rc = min(ts, 128)
    shift = _conv_shift_matrix(rc)
    return pl.pallas_call(
        functools.partial(_conv_kernel, ts=ts, rc=rc),
        grid=(B, nt),
        in_specs=[cur(0), cur(1), cur(2), prev(0), prev(1), nxt(0), nxt(1),
                  full(CONV_K), full(1), full(1), full(1),
                  pl.BlockSpec(shift.shape, lambda b, i: (0, 0))],
        out_specs=pl.BlockSpec((ts, W_A), lambda b, i: (b * nt + i, 0)),
        out_shape=jax.ShapeDtypeStruct((T, W_A), BF16),
        scratch_shapes=[pltpu.VMEM((ts + 2 * HALO, W_A), BF16),
                        pltpu.VMEM((shift.shape[0], W_A), F32)],
        compiler_params=_params(("arbitrary", "arbitrary")),
        name="conv_module",
    )(p, p, p, p, p, p, p, conv_w, conv_b, gn_g, gn_b, shift)


def _mlstm_kernel(qtf_ref, kf_ref, vtf_ref, gcf_ref, grf_ref,
                  qtb_ref, kb_ref, vtb_ref, gcb_ref, grb_ref,
                  hf_ref, hb_ref, st_scr, m_scr, *, L):
    c = pl.program_id(1)

    @pl.when(c == 0)
    def _():
        st_scr[...] = jnp.zeros_like(st_scr)
        m_scr[...] = jnp.zeros_like(m_scr)

    row = lax.broadcasted_iota(jnp.int32, (L, L), 0)
    col = lax.broadcasted_iota(jnp.int32, (L, L), 1)
    lower = col <= row
    upper = col >= row
    lower_f = lower.astype(BF16)
    upper_f = upper.astype(BF16)
    scale = DQK_B ** -0.5
    ones_rows = (lax.broadcasted_iota(jnp.int32, (MLSTM_AUG - DV_B, L), 0) == 0).astype(BF16)

    def direction(qt_ref, k_ref, vt_ref, gc_ref, gr_ref, h_ref, fwd):
        gate_i = 0 if fwd else 2 * H_B
        gate_f = gate_i + H_B
        mask_t = upper if fwd else lower
        ls_col = _log_sigmoid(gc_ref[...])
        ls_row = _log_sigmoid(gr_ref[...])
        b_col_all = _tri_sum_left(lower_f if fwd else upper_f, ls_col)
        b_row_all = _tri_sum_right(ls_row, upper_f if fwd else lower_f)
        gc = gc_ref[...]
        gr = gr_ref[...]
        for h in range(H_B):
            sidx = h if fwd else H_B + h
            b_col = b_col_all[:, gate_f + h:gate_f + h + 1]
            b_row = b_row_all[gate_f + h:gate_f + h + 1, :]
            i_col = gc[:, gate_i + h:gate_i + h + 1]
            i_row = gr[gate_i + h:gate_i + h + 1, :]
            qt = (qt_ref[h * DQK_B:(h + 1) * DQK_B, :].astype(F32) * scale).astype(BF16)
            k = k_ref[:, h * DQK_B:(h + 1) * DQK_B]
            vt = jnp.concatenate([vt_ref[h * DV_B:(h + 1) * DV_B, :], ones_rows], axis=0)
            m = m_scr[sidx][:, 0:1]
            st = st_scr[sidx]

            d_t = jnp.where(mask_t, b_row + (i_col - b_col), -jnp.inf)
            a = b_row + m
            m_t = jnp.maximum(a, jnp.max(d_t, axis=0, keepdims=True))
            dw = jnp.exp(d_t - m_t)
            aw = jnp.exp(a - m_t)
            s_t = _dot(k, qt) * dw
            num = _dot(vt, s_t.astype(BF16)) + aw * _dot(st.astype(BF16), qt)
            den = num[DV_B:DV_B + 1, :]
            hval = num[0:DV_B, :] / jnp.maximum(jnp.abs(den), jnp.exp(-m_t))
            h_ref[h * DV_B:(h + 1) * DV_B, :] = hval.astype(BF16)

            b_end = b_col[L - 1:L, :] if fwd else b_col[0:1, :]
            g_row = b_end - b_row + i_row
            m_new = jnp.maximum(b_end + m, jnp.max(g_row, axis=1, keepdims=True))
            wk = jnp.exp(g_row - m_new)
            decay = jnp.exp(b_end + m - m_new)
            st_scr[sidx] = decay * st + _dot((vt.astype(F32) * wk).astype(BF16), k)
            m_scr[sidx] = jnp.broadcast_to(m_new, (1, LANES))

    direction(qtf_ref, kf_ref, vtf_ref, gcf_ref, grf_ref, hf_ref, True)
    direction(qtb_ref, kb_ref, vtb_ref, gcb_ref, grb_ref, hb_ref, False)


def _mlstm(p, qt, vt, gates, gates_t, B, S, L):
    T = B * S
    nc = S // L
    kcol = EVEN_K_OFFSET // (H_B * DQK_B)

    def specs(chunk):
        return [
            pl.BlockSpec((H_B * DQK_B, L), lambda b, c: (0, b * nc + chunk(c))),
            pl.BlockSpec((L, H_B * DQK_B), lambda b, c: (b * nc + chunk(c), kcol)),
            pl.BlockSpec((W_B, L), lambda b, c: (0, b * nc + chunk(c))),
            pl.BlockSpec((L, LANES), lambda b, c: (b * nc + chunk(c), 0)),
            pl.BlockSpec((None, 4 * H_B, L), lambda b, c: (b, 0, chunk(c))),
        ]

    fwd = lambda c: c
    bwd = lambda c: nc - 1 - c
    n_streams = 2 * H_B
    return pl.pallas_call(
        functools.partial(_mlstm_kernel, L=L),
        grid=(B, nc),
        in_specs=specs(fwd) + specs(bwd),
        out_specs=[
            pl.BlockSpec((W_B, L), lambda b, c: (0, b * nc + c)),
            pl.BlockSpec((W_B, L), lambda b, c: (0, b * nc + nc - 1 - c)),
        ],
        out_shape=[jax.ShapeDtypeStruct((W_B, T), BF16)] * 2,
        scratch_shapes=[
            pltpu.VMEM((n_streams, MLSTM_AUG, DQK_B), F32),
            pltpu.VMEM((n_streams, 1, LANES), F32),
        ],
        compiler_params=_params(("arbitrary", "arbitrary")),
        name="mlstm",
    )(qt, p, vt, gates, gates_t, qt, p, vt, gates, gates_t)


def _even_out_kernel(ya_ref, hf_ref, hb_ref, o_ref, zb_ref, x_ref, hn_ref, w_ref, g_ref, out_ref):
    parts = []
    for h in range(H_B):
        cs = slice(h * DV_B, (h + 1) * DV_B)
        hm_t = hf_ref[cs, :].astype(F32) + hb_ref[cs, :].astype(F32)
        hm_t = hm_t * lax.rsqrt(jnp.mean(hm_t * hm_t, axis=0, keepdims=True) + RMS_EPS)
        hm = hm_t.T * hn_ref[:, cs]
        yb = _sigmoid(o_ref[:, cs].astype(F32)) * hm * _silu(zb_ref[:, cs].astype(F32))
        parts.append(yb.astype(BF16))
    y = _dot(ya_ref[...], w_ref[0:W_A, :])
    for h in range(H_B):
        y = y + _dot(parts[h], w_ref[W_A + h * DV_B:W_A + (h + 1) * DV_B, :])
    out_ref[...] = x_ref[...] + _rms(y, g_ref[...])


def _even_out(ya, hf, hb, p, x2, hn_g, w_out, g_post, tm):
    T = x2.shape[0]
    ocol = EVEN_O_OFFSET // W_B
    row = lambda w: pl.BlockSpec((tm, w), lambda i: (i, 0))
    feat = pl.BlockSpec((W_B, tm), lambda i: (0, i))
    return pl.pallas_call(
        _even_out_kernel,
        grid=(T // tm,),
        in_specs=[row(W_A), feat, feat,
                  pl.BlockSpec((tm, W_B), lambda i: (i, ocol)),
                  pl.BlockSpec((tm, W_B), lambda i: (i, ocol + 1)),
                  row(D_MODEL),
                  pl.BlockSpec((1, W_B), lambda i: (0, 0)),
                  pl.BlockSpec((W_A + W_B, D_MODEL), lambda i: (0, 0)),
                  pl.BlockSpec((1, D_MODEL), lambda i: (0, 0))],
        out_specs=row(D_MODEL),
        out_shape=jax.ShapeDtypeStruct((T, D_MODEL), F32),
        compiler_params=_params(("arbitrary",)),
        name="even_out_proj",
    )(ya, hf, hb, p, p, x2, hn_g, w_out, g_post)


def _rope128(x, cc, ss):
    lane = lax.broadcasted_iota(jnp.int32, x.shape, 1)
    swapped = jnp.where(lane % QK_ROPE < QK_ROPE // 2,
                        pltpu.roll(x, LANES - QK_ROPE // 2, 1),
                        pltpu.roll(x, QK_ROPE // 2, 1))
    return x * cc + swapped * ss


def _odd_in_kernel(x_ref, g_ref, w_ref, gq_ref, wuqt_ref, gkv_ref, wuk_ref, wuvt_ref,
                   cc_ref, ss_ref, cost_ref, sint_ref,
                   qt_ref, k_ref, vt_ref, f_ref, zc_ref, zd_ref):
    tm = x_ref.shape[0]
    h = _rms(x_ref[...], g_ref[...]).astype(BF16)
    o_ckv = Q_LORA
    o_f = o_ckv + KV_LORA
    o_zc = o_f + W_D
    o_zd = o_zc + W_C
    o_kr = o_zd + W_D
    half = QK_ROPE // 2

    for g in range(D_GROUPS):
        f_ref[g] = _dot(h, w_ref[:, o_f + g * D_GROUP_CH:o_f + (g + 1) * D_GROUP_CH]).astype(BF16)
    zc_ref[...] = _dot(h, w_ref[:, o_zc:o_zc + W_C]).astype(BF16)
    zd_ref[...] = _dot(h, w_ref[:, o_zd:o_zd + W_D]).astype(BF16)

    ckv = _rms(_dot(h, w_ref[:, o_ckv:o_ckv + KV_LORA]), gkv_ref[...]).astype(BF16)
    lane = lax.broadcasted_iota(jnp.int32, (tm, LANES), 1)
    first_half = lane < QK_ROPE
    kr = _rope128(_dot(h, w_ref[:, o_kr:o_kr + LANES]), cc_ref[...], ss_ref[...])
    kr_even = jnp.where(first_half, kr, 0.0).astype(BF16)
    kr_odd = jnp.where(first_half, 0.0, kr).astype(BF16)
    for hd in range(H_C):
        k_ref[hd, :, 0:QK_NOPE] = _dot(ckv, wuk_ref[:, hd * QK_NOPE:(hd + 1) * QK_NOPE]).astype(BF16)
        k_ref[hd, :, QK_NOPE:QK_PAD] = kr_even if hd % 2 == 0 else kr_odd

    vt = _dot_nt(wuvt_ref[...], ckv)
    ones_rows = (lax.broadcasted_iota(jnp.int32, (V_AUG - V_HEAD, tm), 0) == 0).astype(BF16)
    for hd in range(H_C):
        vt_ref[hd, 0:V_HEAD, :] = vt[hd * V_HEAD:(hd + 1) * V_HEAD, :].astype(BF16)
        vt_ref[hd, V_HEAD:V_AUG, :] = ones_rows

    scale = (QK_NOPE + QK_ROPE) ** -0.5 * math.log2(math.e)
    cq = _rms(_dot(h, w_ref[:, 0:Q_LORA]), gq_ref[...]).astype(BF16)
    qn = _dot_nt(wuqt_ref[0:H_C * QK_NOPE, :], cq) * scale
    qr = _dot_nt(wuqt_ref[H_C * QK_NOPE:H_C * (QK_NOPE + QK_ROPE), :], cq) * scale
    ct = cost_ref[...]
    st = sint_ref[...]
    zeros = jnp.zeros((QK_ROPE, tm), BF16)
    for hd in range(H_C):
        qt_ref[hd, 0:QK_NOPE, :] = qn[hd * QK_NOPE:(hd + 1) * QK_NOPE, :].astype(BF16)
        x1 = qr[hd * QK_ROPE:hd * QK_ROPE + half, :]
        x2 = qr[hd * QK_ROPE + half:(hd + 1) * QK_ROPE, :]
        lo = QK_NOPE if hd % 2 == 0 else QK_NOPE + QK_ROPE
        pad = QK_NOPE + QK_ROPE if hd % 2 == 0 else QK_NOPE
        qt_ref[hd, lo:lo + half, :] = (x1 * ct - x2 * st).astype(BF16)
        qt_ref[hd, lo + half:lo + QK_ROPE, :] = (x1 * st + x2 * ct).astype(BF16)
        qt_ref[hd, pad:pad + QK_ROPE, :] = zeros


def _odd_in(x2, g_pre, w_in, g_q, w_uq_t, g_kv, w_uk, w_uv_t, cc, ss, cos_t, sin_t, B, S, tm):
    T = B * S
    nt = S // tm
    full = lambda a: pl.BlockSpec(a.shape, lambda i: (0,) * a.ndim)
    row = lambda w: pl.BlockSpec((tm, w), lambda i: (i, 0))
    return pl.pallas_call(
        _odd_in_kernel,
        grid=(T // tm,),
        in_specs=[row(D_MODEL), full(g_pre), full(w_in), full(g_q), full(w_uq_t), full(g_kv), full(w_uk),
                  full(w_uv_t), row(LANES), row(LANES),
                  pl.BlockSpec((QK_ROPE // 2, tm), lambda i: (0, i)),
                  pl.BlockSpec((QK_ROPE // 2, tm), lambda i: (0, i))],
        out_specs=[pl.BlockSpec((None, H_C, QK_PAD, tm), lambda i: (i // nt, 0, 0, i % nt)),
                   pl.BlockSpec((None, H_C, tm, QK_PAD), lambda i: (i // nt, 0, i % nt, 0)),
                   pl.BlockSpec((None, H_C, None, V_AUG, tm), lambda i: (i // nt, 0, i % nt, 0, 0)),
                   pl.BlockSpec((D_GROUPS, None, tm, D_GROUP_CH), lambda i: (0, i // nt, i % nt, 0)),
                   row(W_C), row(W_D)],
        out_shape=[
            jax.ShapeDtypeStruct((B, H_C, QK_PAD, S), BF16),
            jax.ShapeDtypeStruct((B, H_C, S, QK_PAD), BF16),
            jax.ShapeDtypeStruct((B, H_C, nt, V_AUG, tm), BF16),
            jax.ShapeDtypeStruct((D_GROUPS, B, S, D_GROUP_CH), BF16),
            jax.ShapeDtypeStruct((T, W_C), BF16),
            jax.ShapeDtypeStruct((T, W_D), BF16),
        ],
        compiler_params=_params(("arbitrary",)),
        name="odd_in_proj",
    )(x2, g_pre, w_in, g_q, w_uq_t, g_kv, w_uk, w_uv_t, cc, ss, cos_t, sin_t)


def _attn_kernel(qt_ref, k_ref, vt_ref, o_ref, s_a, s_b, p_a, p_b, acc_scr, *, tk):
    nk = k_ref.shape[0] // tk
    qt = qt_ref[...]
    tq = qt.shape[1]

    s_bufs = (s_a, s_b)
    p_bufs = (p_a, p_b)

    def scores(j):
        return _dot(k_ref[j * tk:(j + 1) * tk, :], qt)

    tv = vt_ref.shape[-1]

    def v_tile(j):
        return vt_ref[j * tk // tv][:, j * tk % tv:j * tk % tv + tk]

    s_bufs[0][...] = scores(0)
    m = None
    alpha_prev = None
    for j in range(nk):
        if j + 1 < nk:
            s_bufs[(j + 1) % 2][...] = scores(j + 1)
        s = s_bufs[j % 2][...]
        tile_max = jnp.max(s, axis=0, keepdims=True)
        m_new = tile_max if m is None else jnp.maximum(m, tile_max)
        if j >= 1:
            pv = _dot(v_tile(j - 1), p_bufs[(j - 1) % 2][...])
            acc_scr[...] = pv if j == 1 else alpha_prev * acc_scr[...] + pv
        alpha_prev = None if m is None else jnp.exp2(m - m_new)
        p_bufs[j % 2][...] = jnp.exp2((s - m_new).astype(BF16))
        m = m_new
    acc = alpha_prev * acc_scr[...] + _dot(v_tile(nk - 1), p_bufs[(nk - 1) % 2][...])
    o_ref[...] = (acc[0:V_HEAD, :] / acc[V_HEAD:V_HEAD + 1, :]).T.astype(BF16)


def _attention(qt, k, vt, B, S, tq, tk):
    nv, tv = vt.shape[2], vt.shape[4]
    assert S // tk >= 2 and tv % tk == 0
    return pl.pallas_call(
        functools.partial(_attn_kernel, tk=tk),
        grid=(B, H_C, S // tq),
        in_specs=[
            pl.BlockSpec((None, None, QK_PAD, tq), lambda b, h, i: (b, h, 0, i)),
            pl.BlockSpec((None, None, S, QK_PAD), lambda b, h, i: (b, h, 0, 0)),
            pl.BlockSpec((None, None, nv, V_AUG, tv), lambda b, h, i: (b, h, 0, 0, 0)),
        ],
        out_specs=pl.BlockSpec((None, tq, V_HEAD), lambda b, h, i: (b, i, h)),
        out_shape=jax.ShapeDtypeStruct((B, S, W_C), BF16),
        scratch_shapes=[pltpu.VMEM((tk, tq), F32), pltpu.VMEM((tk, tq), F32),
                        pltpu.VMEM((tk, tq), BF16), pltpu.VMEM((tk, tq), BF16),
                        pltpu.VMEM((V_AUG, tq), F32)],
        compiler_params=_params(("arbitrary", "arbitrary", "arbitrary")),
        name="flash_attention",
    )(qt, k, vt)


def _fft_tables(S):
    n1 = S // LANES
    a = np.arange(n1, dtype=np.float64)
    ang1 = 2.0 * np.pi * np.outer(a, a) / n1
    w1 = np.concatenate([np.cos(ang1), -np.sin(ang1)], axis=0)
    k1 = np.arange(n1)[:, None, None]
    k2 = np.arange(LANES)[None, :, None]
    n2 = np.arange(LANES)[None, None, :]
    kk = (k1 + n1 * k2) * n2 % S
    ang2 = 2.0 * np.pi * kk.astype(np.float64) / S
    gc = np.cos(ang2).reshape(n1 * LANES, LANES)
    gs = np.sin(ang2).reshape(n1 * LANES, LANES)
    c = np.arange(D_GROUP_CH, dtype=np.float64)
    angc = 2.0 * np.pi * np.outer(c, c) / D_GROUP_CH
    f32 = lambda t: jnp.asarray(t.astype(np.float32))
    return f32(w1), f32(gc), f32(gs), f32(np.cos(angc)), f32(np.sin(angc))


def _fft_kernel(x_ref, w1_ref, gc_ref, gs_ref, cc_ref, sc_ref, out_ref, ar_scr, ai_scr, *, n1, cw):
    n_chunks = (LANES * LANES) // cw
    per = cw // LANES
    w1 = w1_ref[...]

    def stage1(ch, carry):
        lo = pl.multiple_of(ch * cw, cw)
        res = _dot(w1, x_ref[:, pl.ds(lo, cw)])
        for j in range(per):
            r0 = pl.multiple_of((ch * per + j) * n1, n1)
            ar_scr[pl.ds(r0, n1), :] = res[0:n1, j * LANES:(j + 1) * LANES]
            ai_scr[pl.ds(r0, n1), :] = res[n1:2 * n1, j * LANES:(j + 1) * LANES]
        return carry

    lax.fori_loop(0, n_chunks, stage1, 0)

    ccm = cc_ref[...]
    scm = sc_ref[...]
    norm = 1.0 / math.sqrt(n1 * LANES * D_GROUP_CH)

    def stage2_one(k1):
        ar = ar_scr[pl.ds(k1, LANES, stride=n1), :]
        ai = ai_scr[pl.ds(k1, LANES, stride=n1), :]
        a = jnp.concatenate([ar, ai], axis=1).astype(BF16)
        t0 = pl.multiple_of(k1 * LANES, LANES)
        p1 = _dot(gc_ref[pl.ds(t0, LANES), :], a)
        p2 = _dot(gs_ref[pl.ds(t0, LANES), :], a)
        zr = p1[:, 0:LANES] + p2[:, LANES:2 * LANES]
        zi = p1[:, LANES:2 * LANES] - p2[:, 0:LANES]
        fr = _dot(zr.astype(BF16), ccm) + _dot(zi.astype(BF16), scm)
        out_ref[pl.ds(k1, LANES, stride=n1), :] = fr * norm

    group = 4 if n1 % 4 == 0 else 1

    def stage2(i, carry):
        for u in range(group):
            stage2_one(i * group + u)
        return carry

    lax.fori_loop(0, n1 // group, stage2, 0)


def _fft(f4, B, S):
    n1 = S // LANES
    w1, gc, gs, cc, sc = _fft_tables(S)
    bf = lambda t: t.astype(BF16)
    x4 = f4.reshape(D_GROUPS, B, n1, LANES * D_GROUP_CH)
    cw = 2048
    full = lambda a: pl.BlockSpec(a.shape, lambda g, b: (0,) * a.ndim)
    tabs = [bf(w1), bf(gc), bf(gs), bf(cc), bf(sc)]
    return pl.pallas_call(
        functools.partial(_fft_kernel, n1=n1, cw=cw),
        grid=(D_GROUPS, B),
        in_specs=[pl.BlockSpec((None, None, n1, LANES * D_GROUP_CH), lambda g, b: (g, b, 0, 0))]
                 + [full(t) for t in tabs],
        out_specs=pl.BlockSpec((None, S, D_GROUP_CH), lambda g, b: (b, 0, g)),
        out_shape=jax.ShapeDtypeStruct((B, S, W_D), F32),
        scratch_shapes=[pltpu.VMEM((S, D_GROUP_CH), F32), pltpu.VMEM((S, D_GROUP_CH), F32)],
        compiler_params=_params(("arbitrary", "arbitrary")),
        name="fft2_real",
    )(x4, *tabs)


def _odd_out_kernel(att_ref, zc_ref, fr_ref, zd_ref, x_ref, wfd_ref, w_ref, g_ref, out_ref):
    yc = (att_ref[...].astype(F32) * _silu(zc_ref[...].astype(F32))).astype(BF16)
    yd = (_dot(fr_ref[...].astype(BF16), wfd_ref[...]) * _silu(zd_ref[...].astype(F32))).astype(BF16)
    y = _dot(yc, w_ref[0:W_C, :]) + _dot(yd, w_ref[W_C:W_C + W_D, :])
    out_ref[...] = x_ref[...] + _rms(y, g_ref[...])


def _odd_out(att, zc, fr, zd, x2, w_fd, w_out, g_post, tm):
    T = x2.shape[0]
    row = lambda w: pl.BlockSpec((tm, w), lambda i: (i, 0))
    full = lambda a: pl.BlockSpec(a.shape, lambda i: (0,) * a.ndim)
    return pl.pallas_call(
        _odd_out_kernel,
        grid=(T // tm,),
        in_specs=[row(W_C), row(W_C), row(W_D), row(W_D), row(D_MODEL), full(w_fd), full(w_out), full(g_post)],
        out_specs=row(D_MODEL),
        out_shape=jax.ShapeDtypeStruct((T, D_MODEL), F32),
        compiler_params=_params(("arbitrary",)),
        name="odd_out_proj",
    )(att, zc, fr, zd, x2, w_fd, w_out, g_post)


def _even_layer(x2, B, S, g_pre, g_post, w_in, b_gate, conv_w, conv_b, gn_g, gn_b, hn_g, w_out):
    o = np.cumsum([0, 3 * W_A, H_B * DQK_B, H_B * DQK_B, W_B, 2 * W_B, 4 * H_B])
    seg = lambda i: w_in[:, o[i]:o[i + 1]]
    w_main = jnp.concatenate([seg(0), seg(4), seg(2)], axis=1).astype(BF16)
    w_qt = seg(1).T.astype(BF16)
    w_vt = seg(3).T.astype(BF16)
    w_gate = jnp.pad(seg(5), ((0, 0), (0, LANES - 4 * H_B))).astype(BF16)
    bg = jnp.pad(b_gate, (0, LANES - 4 * H_B)).reshape(1, LANES)
    tm = min(1024, S)
    p, qt, vt, gates = _even_in(x2, g_pre.reshape(1, -1), w_main, w_qt, w_vt, w_gate, bg, tm, EVEN_MAIN // 4)
    ya = _conv_module(p, conv_w, conv_b.reshape(1, -1), gn_g.reshape(1, -1), gn_b.reshape(1, -1),
                      B, S, min(256, S))
    gates_t = gates[:, :4 * H_B].reshape(B, S, 4 * H_B).transpose(0, 2, 1)
    hf, hb = _mlstm(p, qt, vt, gates, gates_t, B, S, min(MLSTM_CHUNK, S))
    return _even_out(ya, hf, hb, p, x2, hn_g.reshape(1, -1), w_out.astype(BF16), g_post.reshape(1, -1),
                     min(512, S))


def _odd_layer(x2, B, S, cos, sin, g_pre, g_post, w_in, g_q, w_uq, g_kv, w_ukv, w_fd, w_out):
    T = B * S
    o = np.cumsum([0, Q_LORA, KV_LORA, QK_ROPE, W_D, W_C, W_D])
    seg = lambda i: w_in[:, o[i]:o[i + 1]]
    w_in_p = jnp.concatenate([seg(0), seg(1), seg(3), seg(4), seg(5), seg(2), seg(2)], axis=1).astype(BF16)
    wq = w_uq.reshape(Q_LORA, H_C, QK_NOPE + QK_ROPE)
    w_uq_t = jnp.concatenate([wq[:, :, :QK_NOPE].reshape(Q_LORA, -1),
                              wq[:, :, QK_NOPE:].reshape(Q_LORA, -1)], axis=1).T.astype(BF16)
    wkv = w_ukv.reshape(KV_LORA, H_C, QK_NOPE + V_HEAD)
    w_uk = wkv[:, :, :QK_NOPE].reshape(KV_LORA, -1).astype(BF16)
    w_uv_t = wkv[:, :, QK_NOPE:].reshape(KV_LORA, -1).T.astype(BF16)
    cos2 = cos.reshape(T, QK_ROPE // 2)
    sin2 = sin.reshape(T, QK_ROPE // 2)
    cc = jnp.tile(cos2, (1, 4))
    ss = jnp.tile(jnp.concatenate([-sin2, sin2], axis=-1), (1, 2))
    tm = min(512, S // 2)
    qt, k, vt, f4, zc, zd = _odd_in(x2, g_pre.reshape(1, -1), w_in_p, g_q.reshape(1, -1), w_uq_t,
                                    g_kv.reshape(1, -1), w_uk, w_uv_t, cc, ss, cos2.T, sin2.T, B, S, tm)
    att = _attention(qt, k, vt, B, S, min(ATTN_TQ, S), min(ATTN_TK, tm)).reshape(T, W_C)
    fr = _fft(f4, B, S).reshape(T, W_D)
    return _odd_out(att, zc, fr, zd, x2, w_fd.astype(BF16), w_out.astype(BF16), g_post.reshape(1, -1), tm)


def _rope_tables(positions):
    inv = ROPE_THETA ** (-jnp.arange(0, QK_ROPE, 2, dtype=F32) / QK_ROPE)
    ang = positions.astype(F32)[..., None] * inv
    return jnp.cos(ang), jnp.sin(ang)


def kernel(x, positions, even_g_pre, even_g_post, even_w_in, even_b_gate, even_conv_w, even_conv_b,
           even_gn_g, even_gn_b, even_hn_g, even_w_out, odd_g_pre, odd_g_post, odd_w_in, odd_g_q,
           odd_w_uq, odd_g_kv, odd_w_ukv, odd_w_fd, odd_w_out):
    B, S, _ = x.shape
    cos, sin = _rope_tables(positions)
    h = x.reshape(B * S, D_MODEL)
    depth = even_w_in.shape[0] + odd_w_in.shape[0]
    for layer in range(depth):
        j = layer // 2
        if layer % 2 == 0:
            h = _even_layer(h, B, S, even_g_pre[j], even_g_post[j], even_w_in[j], even_b_gate[j],
                            even_conv_w[j], even_conv_b[j], even_gn_g[j], even_gn_b[j], even_hn_g[j],
                            even_w_out[j])
        else:
            h = _odd_layer(h, B, S, cos, sin, odd_g_pre[j], odd_g_post[j], odd_w_in[j], odd_g_q[j],
                           odd_w_uq[j], odd_g_kv[j], odd_w_ukv[j], odd_w_fd[j], odd_w_out[j])
    return h.reshape(B, S, D_MODEL)
```

```python
import functools
import math

import numpy as np
import jax
import jax.numpy as jnp
from jax import lax
from jax.experimental import pallas as pl
from jax.experimental.pallas import tpu as pltpu

F32 = jnp.float32
BF16 = jnp.bfloat16

D_MODEL = 1024
RMS_EPS = 1e-6
LN_EPS = 1e-5
W_A = 1024
CONV_K = 31
A_GROUPS = 8
H_B = 4
DQK_B = 128
DV_B = 256
W_B = 1024
H_C = 8
QK_NOPE = 128
QK_ROPE = 64
V_HEAD = 128
Q_LORA = 384
KV_LORA = 256
W_C = 1024
ROPE_THETA = 10000.0
D_GROUPS = 4
D_GROUP_CH = 128
W_D = 512

LANES = 128
HALO = 16
QK_PAD = 256
V_AUG = V_HEAD + 16
VMEM_LIMIT = 56 * 1024 * 1024

MLSTM_CHUNK = 256
MLSTM_AUG = DV_B + 16
EVEN_O_OFFSET = 3 * W_A
EVEN_K_OFFSET = 3 * W_A + 2 * W_B
EVEN_MAIN = EVEN_K_OFFSET + H_B * DQK_B
ATTN_TQ = 1024
ATTN_TK = 512
NEG_BIG = -1e30


def _params(sem, vmem=VMEM_LIMIT):
    return pltpu.CompilerParams(dimension_semantics=sem, vmem_limit_bytes=vmem)


def _sigmoid(x):
    return 1.0 / (1.0 + jnp.exp(-x))


def _silu(x):
    return x * _sigmoid(x)


def _log_sigmoid(x):
    return jnp.minimum(x, 0.0) - jnp.log(1.0 + jnp.exp(-jnp.abs(x)))


def _rms(x, g):
    return x * lax.rsqrt(jnp.mean(x * x, axis=-1, keepdims=True) + RMS_EPS) * g


def _dot(a, b):
    return jnp.dot(a, b, preferred_element_type=F32)


def _dot_nt(a, b):
    return lax.dot_general(a, b, (((1,), (1,)), ((), ())), preferred_element_type=F32)


def _split3(x):
    hi = x.astype(BF16)
    r1 = x - hi.astype(F32)
    mid = r1.astype(BF16)
    lo = (r1 - mid.astype(F32)).astype(BF16)
    return hi, mid, lo


def _tri_sum_left(tri, x):
    hi, mid, lo = _split3(x)
    return _dot(tri, lo) + _dot(tri, mid) + _dot(tri, hi)


def _tri_sum_right(x, tri):
    hi, mid, lo = _split3(x)
    return _dot(lo, tri) + _dot(mid, tri) + _dot(hi, tri)


def _even_in_kernel(x_ref, g_ref, w_ref, wqt_ref, wvt_ref, wg_ref, bg_ref, wgt_ref, bgt_ref,
                    p_ref, qt_ref, vt_ref, gates_ref, gates_t_ref, h_scr):
    j = pl.program_id(1)

    @pl.when(j == 0)
    def _():
        h = _rms(x_ref[...], g_ref[...]).astype(BF16)
        h_scr[...] = h
        qt_ref[...] = _dot_nt(wqt_ref[...], h).astype(BF16)
        vt_ref[...] = _dot_nt(wvt_ref[...], h).astype(BF16)
        gates_ref[...] = _dot(h, wg_ref[...]) + bg_ref[...]
        gates_t_ref[...] = _dot_nt(wgt_ref[...], h) + bgt_ref[...]

    p_ref[...] = _dot(h_scr[...], w_ref[...]).astype(BF16)


def _even_in(x2, g_pre, w_main, w_qt, w_vt, w_gate, b_gate, w_gate_t, b_gate_t, tm, tn):
    T = x2.shape[0]
    n_main = w_main.shape[1]
    full = lambda a: pl.BlockSpec(a.shape, lambda i, j: (0, 0))
    return pl.pallas_call(
        _even_in_kernel,
        grid=(T // tm, n_main // tn),
        in_specs=[
            pl.BlockSpec((tm, D_MODEL), lambda i, j: (i, 0)),
            full(g_pre),
            pl.BlockSpec((D_MODEL, tn), lambda i, j: (0, j)),
            full(w_qt), full(w_vt), full(w_gate), full(b_gate), full(w_gate_t), full(b_gate_t),
        ],
        out_specs=[
            pl.BlockSpec((tm, tn), lambda i, j: (i, j)),
            pl.BlockSpec((H_B * DQK_B, tm), lambda i, j: (0, i)),
            pl.BlockSpec((W_B, tm), lambda i, j: (0, i)),
            pl.BlockSpec((tm, LANES), lambda i, j: (i, 0)),
            pl.BlockSpec((4 * H_B, tm), lambda i, j: (0, i)),
        ],
        out_shape=[
            jax.ShapeDtypeStruct((T, n_main), BF16),
            jax.ShapeDtypeStruct((H_B * DQK_B, T), BF16),
            jax.ShapeDtypeStruct((W_B, T), BF16),
            jax.ShapeDtypeStruct((T, LANES), F32),
            jax.ShapeDtypeStruct((4 * H_B, T), F32),
        ],
        scratch_shapes=[pltpu.VMEM((tm, D_MODEL), BF16)],
        compiler_params=_params(("arbitrary", "arbitrary")),
        name="even_in_proj",
    )(x2, g_pre, w_main, w_qt, w_vt, w_gate, b_gate, w_gate_t, b_gate_t)


CONV_FIRST_TAP = HALO - CONV_K // 2
CONV_SUB = 8
CONV_TILE_TAPS = (CONV_FIRST_TAP + CONV_K - 1) // CONV_SUB + 1


def _conv_shift_matrix(rc):
    span = rc + (CONV_TILE_TAPS - 1) * CONV_SUB
    win = rc + 2 * HALO
    m = np.zeros((CONV_SUB * span, win), np.float32)
    for b in range(CONV_SUB):
        m[b * span + np.arange(span), np.arange(span) + b] = 1.0
    return jnp.asarray(m).astype(BF16)


def _conv_kernel(av_ref, ag_ref, za_ref, avp_ref, agp_ref, avn_ref, agn_ref,
                 cw_ref, cb_ref, gg_ref, gb_ref, sm_ref, out_ref, u_scr, sh_scr, *, ts, rc):
    i = pl.program_id(1)
    last = pl.num_programs(1) - 1

    def gated(a_ref, g_ref):
        return a_ref[...].astype(F32) * _sigmoid(g_ref[...].astype(F32))

    u_scr[HALO:HALO + ts, :] = gated(av_ref, ag_ref).astype(BF16)
    u_scr[0:HALO, :] = jnp.where(i > 0, gated(avp_ref, agp_ref), 0.0).astype(BF16)
    u_scr[HALO + ts:HALO + ts + HALO, :] = jnp.where(i < last, gated(avn_ref, agn_ref), 0.0).astype(BF16)

    span = rc + (CONV_TILE_TAPS - 1) * CONV_SUB
    for r0 in range(0, ts, rc):
        sh_scr[...] = _dot(sm_ref[...], u_scr[r0:r0 + rc + 2 * HALO, :])
        for g in range(A_GROUPS):
            cs = slice(g * LANES, (g + 1) * LANES)
            acc = jnp.zeros((rc, LANES), F32) + cb_ref[:, cs]
            for b in range(CONV_SUB):
                for a in range(CONV_TILE_TAPS):
                    j = CONV_SUB * a + b - CONV_FIRST_TAP
                    if 0 <= j < CONV_K:
                        lo = b * span + CONV_SUB * a
                        acc = acc + sh_scr[lo:lo + rc, cs] * cw_ref[j:j + 1, cs]
            mu = jnp.mean(acc, axis=-1, keepdims=True)
            xc = acc - mu
            var = jnp.mean(xc * xc, axis=-1, keepdims=True)
            y = xc * lax.rsqrt(var + LN_EPS) * gg_ref[:, cs] + gb_ref[:, cs]
            y = _silu(y) * _silu(za_ref[r0:r0 + rc, cs].astype(F32))
            out_ref[r0:r0 + rc, cs] = y.astype(BF16)


def _conv_module(p, conv_w, conv_b, gn_g, gn_b, B, S, ts):
    T = B * S
    nt = S // ts
    hb = ts // HALO
    n_hblk = T // HALO

    def cur(col):
        return pl.BlockSpec((ts, W_A), lambda b, i: (b * nt + i, col))

    def prev(col):
        return pl.BlockSpec((HALO, W_A), lambda b, i: (jnp.maximum((b * nt + i) * hb - 1, 0), col))

    def nxt(col):
        return pl.BlockSpec((HALO, W_A), lambda b, i: (jnp.minimum((b * nt + i + 1) * hb, n_hblk - 1), col))

    def full(r):
        return pl.BlockSpec((r, W_A), lambda b, i: (0, 0))

    rc = min(ts, 128)
    shift = _conv_shift_matrix(rc)
    return pl.pallas_call(
        functools.partial(_conv_kernel, ts=ts, rc=rc),
        grid=(B, nt),
        in_specs=[cur(0), cur(1), cur(2), prev(0), prev(1), nxt(0), nxt(1),
                  full(CONV_K), full(1), full(1), full(1),
                  pl.BlockSpec(shift.shape, lambda b, i: (0, 0))],
        out_specs=pl.BlockSpec((ts, W_A), lambda b, i: (b * nt + i, 0)),
        out_shape=jax.ShapeDtypeStruct((T, W_A), BF16),
        scratch_shapes=[pltpu.VMEM((ts + 2 * HALO, W_A), BF16),
                        pltpu.VMEM((shift.shape[0], W_A), F32)],
        compiler_params=_params(("arbitrary", "arbitrary")),
        name="conv_module",
    )(p, p, p, p, p, p, p, conv_w, conv_b, gn_g, gn_b, shift)


def _mlstm_kernel(qtf_ref, kf_ref, vtf_ref, gcf_ref, grf_ref,
                  qtb_ref, kb_ref, vtb_ref, gcb_ref, grb_ref,
                  hf_ref, hb_ref, st_scr, m_scr, *, L):
    c = pl.program_id(1)

    @pl.when(c == 0)
    def _():
        st_scr[...] = jnp.zeros_like(st_scr)
        m_scr[...] = jnp.zeros_like(m_scr)

    row = lax.broadcasted_iota(jnp.int32, (L, L), 0)
    col = lax.broadcasted_iota(jnp.int32, (L, L), 1)
    lower = col <= row
    upper = col >= row
    lower_f = lower.astype(BF16)
    upper_f = upper.astype(BF16)
    scale = DQK_B ** -0.5
    ones_rows = (lax.broadcasted_iota(jnp.int32, (MLSTM_AUG - DV_B, L), 0) == 0).astype(BF16)

    def direction(qt_ref, k_ref, vt_ref, gc_ref, gr_ref, h_ref, fwd):
        gate_i = 0 if fwd else 2 * H_B
        gate_f = gate_i + H_B
        mask_t = upper if fwd else lower
        ls_col = _log_sigmoid(gc_ref[...])
        ls_row = _log_sigmoid(gr_ref[...])
        b_col_all = _tri_sum_left(lower_f if fwd else upper_f, ls_col)
        b_row_all = _tri_sum_right(ls_row, upper_f if fwd else lower_f)
        gc = gc_ref[...]
        gr = gr_ref[...]
        for h in range(H_B):
            sidx = h if fwd else H_B + h
            b_col = b_col_all[:, gate_f + h:gate_f + h + 1]
            b_row = b_row_all[gate_f + h:gate_f + h + 1, :]
            i_col = gc[:, gate_i + h:gate_i + h + 1]
            i_row = gr[gate_i + h:gate_i + h + 1, :]
            qt = (qt_ref[h * DQK_B:(h + 1) * DQK_B, :].astype(F32) * scale).astype(BF16)
            k = k_ref[:, h * DQK_B:(h + 1) * DQK_B]
            vt = jnp.concatenate([vt_ref[h * DV_B:(h + 1) * DV_B, :], ones_rows], axis=0)
            m = m_scr[sidx][:, 0:1]
            st = st_scr[sidx]

            d_t = jnp.where(mask_t, b_row + (i_col - b_col), -jnp.inf)
            a = b_row + m
            m_t = jnp.maximum(a, jnp.max(d_t, axis=0, keepdims=True))
            dw = jnp.exp(d_t - m_t)
            aw = jnp.exp(a - m_t)
            s_t = _dot(k, qt) * dw
            num = _dot(vt, s_t.astype(BF16)) + aw * _dot(st.astype(BF16), qt)
            den = num[DV_B:DV_B + 1, :]
            hval = num[0:DV_B, :] / jnp.maximum(jnp.abs(den), jnp.exp(-m_t))
            h_ref[h * DV_B:(h + 1) * DV_B, :] = hval.astype(BF16)

            b_end = b_col[L - 1:L, :] if fwd else b_col[0:1, :]
            g_row = b_end - b_row + i_row
            m_new = jnp.maximum(b_end + m, jnp.max(g_row, axis=1, keepdims=True))
            wk = jnp.exp(g_row - m_new)
            decay = jnp.exp(b_end + m - m_new)
            st_scr[sidx] = decay * st + _dot((vt.astype(F32) * wk).astype(BF16), k)
            m_scr[sidx] = jnp.broadcast_to(m_new, (1, LANES))

    direction(qtf_ref, kf_ref, vtf_ref, gcf_ref, grf_ref, hf_ref, True)
    direction(qtb_ref, kb_ref, vtb_ref, gcb_ref, grb_ref, hb_ref, False)


def _mlstm(p, qt, vt, gates, gates_t, B, S, L):
    T = B * S
    nc = S // L
    kcol = EVEN_K_OFFSET // (H_B * DQK_B)

    def specs(chunk):
        return [
            pl.BlockSpec((H_B * DQK_B, L), lambda b, c: (0, b * nc + chunk(c))),
            pl.BlockSpec((L, H_B * DQK_B), lambda b, c: (b * nc + chunk(c), kcol)),
            pl.BlockSpec((W_B, L), lambda b, c: (0, b * nc + chunk(c))),
            pl.BlockSpec((L, LANES), lambda b, c: (b * nc + chunk(c), 0)),
            pl.BlockSpec((4 * H_B, L), lambda b, c: (0, b * nc + chunk(c))),
        ]

    fwd = lambda c: c
    bwd = lambda c: nc - 1 - c
    n_streams = 2 * H_B
    return pl.pallas_call(
        functools.partial(_mlstm_kernel, L=L),
        grid=(B, nc),
        in_specs=specs(fwd) + specs(bwd),
        out_specs=[
            pl.BlockSpec((W_B, L), lambda b, c: (0, b * nc + c)),
            pl.BlockSpec((W_B, L), lambda b, c: (0, b * nc + nc - 1 - c)),
        ],
        out_shape=[jax.ShapeDtypeStruct((W_B, T), BF16)] * 2,
        scratch_shapes=[
            pltpu.VMEM((n_streams, MLSTM_AUG, DQK_B), F32),
            pltpu.VMEM((n_streams, 1, LANES), F32),
        ],
        compiler_params=_params(("arbitrary", "arbitrary")),
        name="mlstm",
    )(qt, p, vt, gates, gates_t, qt, p, vt, gates, gates_t)


def _even_out_kernel(ya_ref, hf_ref, hb_ref, o_ref, zb_ref, x_ref, hn_ref, w_ref, g_ref, out_ref):
    parts = []
    for h in range(H_B):
        cs = slice(h * DV_B, (h + 1) * DV_B)
        hm_t = hf_ref[cs, :].astype(F32) + hb_ref[cs, :].astype(F32)
        hm_t = hm_t * lax.rsqrt(jnp.mean(hm_t * hm_t, axis=0, keepdims=True) + RMS_EPS)
        hm = hm_t.T * hn_ref[:, cs]
        yb = _sigmoid(o_ref[:, cs].astype(F32)) * hm * _silu(zb_ref[:, cs].astype(F32))
        parts.append(yb.astype(BF16))
    y = _dot(ya_ref[...], w_ref[0:W_A, :])
    for h in range(H_B):
        y = y + _dot(parts[h], w_ref[W_A + h * DV_B:W_A + (h + 1) * DV_B, :])
    out_ref[...] = x_ref[...] + _rms(y, g_ref[...])


def _even_out(ya, hf, hb, p, x2, hn_g, w_out, g_post, tm):
    T = x2.shape[0]
    ocol = EVEN_O_OFFSET // W_B
    row = lambda w: pl.BlockSpec((tm, w), lambda i: (i, 0))
    feat = pl.BlockSpec((W_B, tm), lambda i: (0, i))
    return pl.pallas_call(
        _even_out_kernel,
        grid=(T // tm,),
        in_specs=[row(W_A), feat, feat,
                  pl.BlockSpec((tm, W_B), lambda i: (i, ocol)),
                  pl.BlockSpec((tm, W_B), lambda i: (i, ocol + 1)),
                  row(D_MODEL),
                  pl.BlockSpec((1, W_B), lambda i: (0, 0)),
                  pl.BlockSpec((W_A + W_B, D_MODEL), lambda i: (0, 0)),
                  pl.BlockSpec((1, D_MODEL), lambda i: (0, 0))],
        out_specs=row(D_MODEL),
        out_shape=jax.ShapeDtypeStruct((T, D_MODEL), F32),
        compiler_params=_params(("arbitrary",)),
        name="even_out_proj",
    )(ya, hf, hb, p, p, x2, hn_g, w_out, g_post)


def _rope128(x, cc, ss):
    lane = lax.broadcasted_iota(jnp.int32, x.shape, 1)
    swapped = jnp.where(lane % QK_ROPE < QK_ROPE // 2,
                        pltpu.roll(x, LANES - QK_ROPE // 2, 1),
                        pltpu.roll(x, QK_ROPE // 2, 1))
    return x * cc + swapped * ss


def _odd_in_kernel(x_ref, g_ref, w_ref, gq_ref, wuqt_ref, gkv_ref, wuk_ref, wuvt_ref,
                   cc_ref, ss_ref, cost_ref, sint_ref,
                   qt_ref, k_ref, vt_ref, f_ref, zc_ref, zd_ref):
    tm = x_ref.shape[0]
    h = _rms(x_ref[...], g_ref[...]).astype(BF16)
    o_ckv = Q_LORA
    o_f = o_ckv + KV_LORA
    o_zc = o_f + W_D
    o_zd = o_zc + W_C
    o_kr = o_zd + W_D
    half = QK_ROPE // 2

    f_all = _dot(h, w_ref[:, o_f:o_f + W_D]).astype(BF16)
    for g in range(D_GROUPS):
        f_ref[g] = f_all[:, g * D_GROUP_CH:(g + 1) * D_GROUP_CH]
    zc_ref[...] = _dot(h, w_ref[:, o_zc:o_zc + W_C]).astype(BF16)
    zd_ref[...] = _dot(h, w_ref[:, o_zd:o_zd + W_D]).astype(BF16)

    ckv = _rms(_dot(h, w_ref[:, o_ckv:o_ckv + KV_LORA]), gkv_ref[...]).astype(BF16)
    lane = lax.broadcasted_iota(jnp.int32, (tm, LANES), 1)
    first_half = lane < QK_ROPE
    kr = _rope128(_dot(h, w_ref[:, o_kr:o_kr + LANES]), cc_ref[...], ss_ref[...])
    kr_even = jnp.where(first_half, kr, 0.0).astype(BF16)
    kr_odd = jnp.where(first_half, 0.0, kr).astype(BF16)
    k_all = _dot(ckv, wuk_ref[...]).astype(BF16)
    for hd in range(H_C):
        k_ref[hd, :, 0:QK_NOPE] = k_all[:, hd * QK_NOPE:(hd + 1) * QK_NOPE]
        k_ref[hd, :, QK_NOPE:QK_PAD] = kr_even if hd % 2 == 0 else kr_odd

    vt = _dot_nt(wuvt_ref[...], ckv)
    ones_rows = (lax.broadcasted_iota(jnp.int32, (V_AUG - V_HEAD, tm), 0) == 0).astype(BF16)
    for hd in range(H_C):
        vt_ref[hd, 0:V_HEAD, :] = vt[hd * V_HEAD:(hd + 1) * V_HEAD, :].astype(BF16)
        vt_ref[hd, V_HEAD:V_AUG, :] = ones_rows

    scale = (QK_NOPE + QK_ROPE) ** -0.5 * math.log2(math.e)
    cq = _rms(_dot(h, w_ref[:, 0:Q_LORA]), gq_ref[...]).astype(BF16)
    qn = _dot_nt(wuqt_ref[0:H_C * QK_NOPE, :], cq) * scale
    qr = _dot_nt(wuqt_ref[H_C * QK_NOPE:H_C * (QK_NOPE + QK_ROPE), :], cq) * scale
    ct = cost_ref[...]
    st = sint_ref[...]
    zeros = jnp.zeros((QK_ROPE, tm), BF16)
    for hd in range(H_C):
        qt_ref[hd, 0:QK_NOPE, :] = qn[hd * QK_NOPE:(hd + 1) * QK_NOPE, :].astype(BF16)
        x1 = qr[hd * QK_ROPE:hd * QK_ROPE + half, :]
        x2 = qr[hd * QK_ROPE + half:(hd + 1) * QK_ROPE, :]
        lo = QK_NOPE if hd % 2 == 0 else QK_NOPE + QK_ROPE
        pad = QK_NOPE + QK_ROPE if hd % 2 == 0 else QK_NOPE
        qt_ref[hd, lo:lo + half, :] = (x1 * ct - x2 * st).astype(BF16)
        qt_ref[hd, lo + half:lo + QK_ROPE, :] = (x1 * st + x2 * ct).astype(BF16)
        qt_ref[hd, pad:pad + QK_ROPE, :] = zeros


def _odd_in(x2, g_pre, w_in, g_q, w_uq_t, g_kv, w_uk, w_uv_t, cc, ss, cos_t, sin_t, B, S, tm):
    T = B * S
    nt = S // tm
    full = lambda a: pl.BlockSpec(a.shape, lambda i: (0,) * a.ndim)
    row = lambda w: pl.BlockSpec((tm, w), lambda i: (i, 0))
    return pl.pallas_call(
        _odd_in_kernel,
        grid=(T // tm,),
        in_specs=[row(D_MODEL), full(g_pre), full(w_in), full(g_q), full(w_uq_t), full(g_kv), full(w_uk),
                  full(w_uv_t), row(LANES), row(LANES),
                  pl.BlockSpec((QK_ROPE // 2, tm), lambda i: (0, i)),
                  pl.BlockSpec((QK_ROPE // 2, tm), lambda i: (0, i))],
        out_specs=[pl.BlockSpec((None, H_C, QK_PAD, tm), lambda i: (i // nt, 0, 0, i % nt)),
                   pl.BlockSpec((None, H_C, tm, QK_PAD), lambda i: (i // nt, 0, i % nt, 0)),
                   pl.BlockSpec((None, H_C, None, V_AUG, tm), lambda i: (i // nt, 0, i % nt, 0, 0)),
                   pl.BlockSpec((D_GROUPS, None, tm, D_GROUP_CH), lambda i: (0, i // nt, i % nt, 0)),
                   row(W_C), row(W_D)],
        out_shape=[
            jax.ShapeDtypeStruct((B, H_C, QK_PAD, S), BF16),
            jax.ShapeDtypeStruct((B, H_C, S, QK_PAD), BF16),
            jax.ShapeDtypeStruct((B, H_C, nt, V_AUG, tm), BF16),
            jax.ShapeDtypeStruct((D_GROUPS, B, S, D_GROUP_CH), BF16),
            jax.ShapeDtypeStruct((T, W_C), BF16),
            jax.ShapeDtypeStruct((T, W_D), BF16),
        ],
        compiler_params=_params(("arbitrary",)),
        name="odd_in_proj",
    )(x2, g_pre, w_in, g_q, w_uq_t, g_kv, w_uk, w_uv_t, cc, ss, cos_t, sin_t)


def _attn_kernel(qt_ref, k_ref, vt_ref, o_ref, s_a, s_b, p_a, p_b, acc_scr, *, tk):
    nk = k_ref.shape[0] // tk
    qt = qt_ref[...]
    tq = qt.shape[1]

    s_bufs = (s_a, s_b)
    p_bufs = (p_a, p_b)

    def scores(j):
        return _dot(k_ref[j * tk:(j + 1) * tk, :], qt)

    tv = vt_ref.shape[-1]

    def v_tile(j):
        return vt_ref[j * tk // tv][:, j * tk % tv:j * tk % tv + tk]

    s_bufs[0][...] = scores(0)
    m = None
    alpha_prev = None
    for j in range(nk):
        if j + 1 < nk:
            s_bufs[(j + 1) % 2][...] = scores(j + 1)
        s = s_bufs[j % 2][...]
        tile_max = jnp.max(s, axis=0, keepdims=True)
        m_new = tile_max if m is None else jnp.maximum(m, tile_max)
        if j >= 1:
            pv = _dot(v_tile(j - 1), p_bufs[(j - 1) % 2][...])
            acc_scr[...] = pv if j == 1 else alpha_prev * acc_scr[...] + pv
        alpha_prev = None if m is None else jnp.exp2(m - m_new)
        p_bufs[j % 2][...] = jnp.exp2((s - m_new).astype(BF16))
        m = m_new
    acc = alpha_prev * acc_scr[...] + _dot(v_tile(nk - 1), p_bufs[(nk - 1) % 2][...])
    o_ref[...] = (acc[0:V_HEAD, :] / acc[V_HEAD:V_HEAD + 1, :]).T.astype(BF16)


def _attention(qt, k, vt, B, S, tq, tk):
    nv, tv = vt.shape[2], vt.shape[4]
    assert S // tk >= 2 and tv % tk == 0
    return pl.pallas_call(
        functools.partial(_attn_kernel, tk=tk),
        grid=(B, H_C, S // tq),
        in_specs=[
            pl.BlockSpec((None, None, QK_PAD, tq), lambda b, h, i: (b, h, 0, i)),
            pl.BlockSpec((None, None, S, QK_PAD), lambda b, h, i: (b, h, 0, 0)),
            pl.BlockSpec((None, None, nv, V_AUG, tv), lambda b, h, i: (b, h, 0, 0, 0)),
        ],
        out_specs=pl.BlockSpec((None, tq, V_HEAD), lambda b, h, i: (b, i, h)),
        out_shape=jax.ShapeDtypeStruct((B, S, W_C), BF16),
        scratch_shapes=[pltpu.VMEM((tk, tq), F32), pltpu.VMEM((tk, tq), F32),
                        pltpu.VMEM((tk, tq), BF16), pltpu.VMEM((tk, tq), BF16),
                        pltpu.VMEM((V_AUG, tq), F32)],
        compiler_params=_params(("arbitrary", "arbitrary", "arbitrary")),
        name="flash_attention",
    )(qt, k, vt)


def _fft_tables(S):
    n1 = S // LANES
    a = np.arange(n1, dtype=np.float64)
    ang1 = 2.0 * np.pi * np.outer(a, a) / n1
    w1 = np.concatenate([np.cos(ang1), -np.sin(ang1)], axis=0)
    k1 = np.arange(n1)[:, None, None]
    k2 = np.arange(LANES)[None, :, None]
    n2 = np.arange(LANES)[None, None, :]
    kk = (k1 + n1 * k2) * n2 % S
    ang2 = 2.0 * np.pi * kk.astype(np.float64) / S
    gc = np.cos(ang2).reshape(n1 * LANES, LANES)
    gs = np.sin(ang2).reshape(n1 * LANES, LANES)
    c = np.arange(D_GROUP_CH, dtype=np.float64)
    angc = 2.0 * np.pi * np.outer(c, c) / D_GROUP_CH
    f32 = lambda t: jnp.asarray(t.astype(np.float32))
    return f32(w1), f32(gc), f32(gs), f32(np.cos(angc)), f32(np.sin(angc))


def _fft_pitch(n1):
    return n1 + 8


def _fft_kernel(x_ref, w1_ref, gc_ref, gs_ref, cc_ref, sc_ref, out_ref, ar_scr, ai_scr, z_scr, *, n1, cw):
    n_chunks = (LANES * LANES) // cw
    per = cw // LANES
    pitch = _fft_pitch(n1)
    w1 = w1_ref[...]

    def stage1(ch, carry):
        lo = pl.multiple_of(ch * cw, cw)
        res = _dot(w1, x_ref[:, pl.ds(lo, cw)])
        for j in range(per):
            r0 = pl.multiple_of((ch * per + j) * pitch, 8) if pitch % 8 == 0 else (ch * per + j) * pitch
            ar_scr[pl.ds(r0, n1), :] = res[0:n1, j * LANES:(j + 1) * LANES]
            ai_scr[pl.ds(r0, n1), :] = res[n1:2 * n1, j * LANES:(j + 1) * LANES]
        return carry

    lax.fori_loop(0, n_chunks, stage1, 0)

    ccm = cc_ref[...]
    scm = sc_ref[...]
    norm = 1.0 / math.sqrt(n1 * LANES * D_GROUP_CH)

    def sequence_dft(k1):
        ar = ar_scr[pl.ds(k1, LANES, stride=pitch), :]
        ai = ai_scr[pl.ds(k1, LANES, stride=pitch), :]
        a = jnp.concatenate([ar, ai], axis=1).astype(BF16)
        t0 = pl.multiple_of(k1 * LANES, LANES)
        p1 = _dot(gc_ref[pl.ds(t0, LANES), :], a)
        p2 = _dot(gs_ref[pl.ds(t0, LANES), :], a)
        zr = p1[:, 0:LANES] + p2[:, LANES:2 * LANES]
        zi = p1[:, LANES:2 * LANES] - p2[:, 0:LANES]
        return zr.astype(BF16), zi.astype(BF16)

    def channel_dft(k1, z):
        fr = _dot(z[0], ccm) + _dot(z[1], scm)
        z_scr[pl.ds(k1, LANES, stride=pitch), :] = fr * norm

    group = 8 if n1 % 8 == 0 else 1

    def stage2(i, carry):
        zs = [sequence_dft(i * group + u) for u in range(group)]
        for u in range(group):
            channel_dft(i * group + u, zs[u])
        return carry

    lax.fori_loop(0, n1 // group, stage2, 0)

    for k2 in range(LANES):
        out_ref[k2 * n1:(k2 + 1) * n1, :] = z_scr[k2 * pitch:k2 * pitch + n1, :]


def _fft(f4, B, S):
    n1 = S // LANES
    w1, gc, gs, cc, sc = _fft_tables(S)
    bf = lambda t: t.astype(BF16)
    x4 = f4.reshape(D_GROUPS, B, n1, LANES * D_GROUP_CH)
    cw = 2048
    full = lambda a: pl.BlockSpec(a.shape, lambda g, b: (0,) * a.ndim)
    tabs = [bf(w1), bf(gc), bf(gs), bf(cc), bf(sc)]
    return pl.pallas_call(
        functools.partial(_fft_kernel, n1=n1, cw=cw),
        grid=(D_GROUPS, B),
        in_specs=[pl.BlockSpec((None, None, n1, LANES * D_GROUP_CH), lambda g, b: (g, b, 0, 0))]
                 + [full(t) for t in tabs],
        out_specs=pl.BlockSpec((None, S, D_GROUP_CH), lambda g, b: (b, 0, g)),
        out_shape=jax.ShapeDtypeStruct((B, S, W_D), F32),
        scratch_shapes=[pltpu.VMEM((LANES * _fft_pitch(n1), D_GROUP_CH), F32)] * 3,
        compiler_params=_params(("arbitrary", "arbitrary")),
        name="fft2_real",
    )(x4, *tabs)


def _odd_out_kernel(att_ref, zc_ref, fr_ref, zd_ref, x_ref, wfd_ref, w_ref, g_ref, out_ref):
    yc = (att_ref[...].astype(F32) * _silu(zc_ref[...].astype(F32))).astype(BF16)
    yd = (_dot(fr_ref[...].astype(BF16), wfd_ref[...]) * _silu(zd_ref[...].astype(F32))).astype(BF16)
    y = _dot(yc, w_ref[0:W_C, :]) + _dot(yd, w_ref[W_C:W_C + W_D, :])
    out_ref[...] = x_ref[...] + _rms(y, g_ref[...])


def _odd_out(att, zc, fr, zd, x2, w_fd, w_out, g_post, tm):
    T = x2.shape[0]
    row = lambda w: pl.BlockSpec((tm, w), lambda i: (i, 0))
    full = lambda a: pl.BlockSpec(a.shape, lambda i: (0,) * a.ndim)
    return pl.pallas_call(
        _odd_out_kernel,
        grid=(T // tm,),
        in_specs=[row(W_C), row(W_C), row(W_D), row(W_D), row(D_MODEL), full(w_fd), full(w_out), full(g_post)],
        out_specs=row(D_MODEL),
        out_shape=jax.ShapeDtypeStruct((T, D_MODEL), F32),
        compiler_params=_params(("arbitrary",)),
        name="odd_out_proj",
    )(att, zc, fr, zd, x2, w_fd, w_out, g_post)


def _even_layer(x2, B, S, g_pre, g_post, w_in, b_gate, conv_w, conv_b, gn_g, gn_b, hn_g, w_out):
    o = np.cumsum([0, 3 * W_A, H_B * DQK_B, H_B * DQK_B, W_B, 2 * W_B, 4 * H_B])
    seg = lambda i: w_in[:, o[i]:o[i + 1]]
    w_main = jnp.concatenate([seg(0), seg(4), seg(2)], axis=1).astype(BF16)
    w_qt = seg(1).T.astype(BF16)
    w_vt = seg(3).T.astype(BF16)
    w_gate = jnp.pad(seg(5), ((0, 0), (0, LANES - 4 * H_B))).astype(BF16)
    bg = jnp.pad(b_gate, (0, LANES - 4 * H_B)).reshape(1, LANES)
    tm = min(1024, S)
    p, qt, vt, gates, gates_t = _even_in(x2, g_pre.reshape(1, -1), w_main, w_qt, w_vt, w_gate, bg,
                                         seg(5).T.astype(BF16), b_gate.reshape(-1, 1), tm, EVEN_MAIN // 4)
    ya = _conv_module(p, conv_w, conv_b.reshape(1, -1), gn_g.reshape(1, -1), gn_b.reshape(1, -1),
                      B, S, min(256, S))
    hf, hb = _mlstm(p, qt, vt, gates, gates_t, B, S, min(MLSTM_CHUNK, S))
    return _even_out(ya, hf, hb, p, x2, hn_g.reshape(1, -1), w_out.astype(BF16), g_post.reshape(1, -1),
                     min(512, S))


def _odd_layer(x2, B, S, cos, sin, g_pre, g_post, w_in, g_q, w_uq, g_kv, w_ukv, w_fd, w_out):
    T = B * S
    o = np.cumsum([0, Q_LORA, KV_LORA, QK_ROPE, W_D, W_C, W_D])
    seg = lambda i: w_in[:, o[i]:o[i + 1]]
    w_in_p = jnp.concatenate([seg(0), seg(1), seg(3), seg(4), seg(5), seg(2), seg(2)], axis=1).astype(BF16)
    wq = w_uq.reshape(Q_LORA, H_C, QK_NOPE + QK_ROPE)
    w_uq_t = jnp.concatenate([wq[:, :, :QK_NOPE].reshape(Q_LORA, -1),
                              wq[:, :, QK_NOPE:].reshape(Q_LORA, -1)], axis=1).T.astype(BF16)
    wkv = w_ukv.reshape(KV_LORA, H_C, QK_NOPE + V_HEAD)
    w_uk = wkv[:, :, :QK_NOPE].reshape(KV_LORA, -1).astype(BF16)
    w_uv_t = wkv[:, :, QK_NOPE:].reshape(KV_LORA, -1).T.astype(BF16)
    cos2 = cos.reshape(T, QK_ROPE // 2)
    sin2 = sin.reshape(T, QK_ROPE // 2)
    cc = jnp.tile(cos2, (1, 4))
    ss = jnp.tile(jnp.concatenate([-sin2, sin2], axis=-1), (1, 2))
    tm = min(512, S // 2)
    qt, k, vt, f4, zc, zd = _odd_in(x2, g_pre.reshape(1, -1), w_in_p, g_q.reshape(1, -1), w_uq_t,
                                    g_kv.reshape(1, -1), w_uk, w_uv_t, cc, ss, cos2.T, sin2.T, B, S, tm)
    att = _attention(qt, k, vt, B, S, min(ATTN_TQ, S), min(ATTN_TK, tm)).reshape(T, W_C)
    fr = _fft(f4, B, S).reshape(T, W_D)
    return _odd_out(att, zc, fr, zd, x2, w_fd.astype(BF16), w_out.astype(BF16), g_post.reshape(1, -1), tm)


def _rope_tables(positions):
    inv = ROPE_THETA ** (-jnp.arange(0, QK_ROPE, 2, dtype=F32) / QK_ROPE)
    ang = positions.astype(F32)[..., None] * inv
    return jnp.cos(ang), jnp.sin(ang)


def kernel(x, positions, even_g_pre, even_g_post, even_w_in, even_b_gate, even_conv_w, even_conv_b,
           even_gn_g, even_gn_b, even_hn_g, even_w_out, odd_g_pre, odd_g_post, odd_w_in, odd_g_q,
           odd_w_uq, odd_g_kv, odd_w_ukv, odd_w_fd, odd_w_out):
    B, S, _ = x.shape
    cos, sin = _rope_tables(positions)
    h = x.reshape(B * S, D_MODEL)
    depth = even_w_in.shape[0] + odd_w_in.shape[0]
    for layer in range(depth):
        j = layer // 2
        if layer % 2 == 0:
            h = _even_layer(h, B, S, even_g_pre[j], even_g_post[j], even_w_in[j], even_b_gate[j],
                            even_conv_w[j], even_conv_b[j], even_gn_g[j], even_gn_b[j], even_hn_g[j],
                            even_w_out[j])
        else:
            h = _odd_layer(h, B, S, cos, sin, odd_g_pre[j], odd_g_post[j], odd_w_in[j], odd_g_q[j],
                           odd_w_uq[j], odd_g_kv[j], odd_w_ukv[j], odd_w_fd[j], odd_w_out[j])
    return h.reshape(B, S, D_MODEL)
```

```python
import functools
import math

import numpy as np
import jax
import jax.numpy as jnp
from jax import lax
from jax.experimental import pallas as pl
from jax.experimental.pallas import tpu as pltpu

F32 = jnp.float32
BF16 = jnp.bfloat16

D_MODEL = 1024
RMS_EPS = 1e-6
LN_EPS = 1e-5
W_A = 1024
CONV_K = 31
A_GROUPS = 8
H_B = 4
DQK_B = 128
DV_B = 256
W_B = 1024
H_C = 8
QK_NOPE = 128
QK_ROPE = 64
V_HEAD = 128
Q_LORA = 384
KV_LORA = 256
W_C = 1024
ROPE_THETA = 10000.0
D_GROUPS = 4
D_GROUP_CH = 128
W_D = 512

LANES = 128
HALO = 16
QK_PAD = 256
V_AUG = V_HEAD + 16
VMEM_LIMIT = 56 * 1024 * 1024

MLSTM_CHUNK = 256
MLSTM_AUG = DV_B + 16
ATTN_TQ = 1024
ATTN_TK = 512
NEG_BIG = -1e30


def _params(sem, vmem=VMEM_LIMIT):
    return pltpu.CompilerParams(dimension_semantics=sem, vmem_limit_bytes=vmem)


def _sigmoid(x):
    return 1.0 / (1.0 + jnp.exp(-x))


def _silu(x):
    return x * _sigmoid(x)


def _log_sigmoid(x):
    return jnp.minimum(x, 0.0) - jnp.log(1.0 + jnp.exp(-jnp.abs(x)))


def _rms(x, g):
    return x * lax.rsqrt(jnp.mean(x * x, axis=-1, keepdims=True) + RMS_EPS) * g


def _dot(a, b):
    return jnp.dot(a, b, preferred_element_type=F32)


def _dot_nt(a, b):
    return lax.dot_general(a, b, (((1,), (1,)), ((), ())), preferred_element_type=F32)


def _split3(x):
    hi = x.astype(BF16)
    r1 = x - hi.astype(F32)
    mid = r1.astype(BF16)
    lo = (r1 - mid.astype(F32)).astype(BF16)
    return hi, mid, lo


def _tri_sum_left(tri, x):
    hi, mid, lo = _split3(x)
    return _dot(tri, lo) + _dot(tri, mid) + _dot(tri, hi)


def _tri_sum_right(x, tri):
    hi, mid, lo = _split3(x)
    return _dot(lo, tri) + _dot(mid, tri) + _dot(hi, tri)


def _even_in_kernel(x_ref, g_ref, wa_ref, woz_ref, wk_ref, wqt_ref, wvt_ref, wg_ref, bg_ref, wgt_ref, bgt_ref,
                    pa_ref, poz_ref, pk_ref, qt_ref, vt_ref, gates_ref, gates_t_ref):
    h = _rms(x_ref[...], g_ref[...]).astype(BF16)
    chunk = 1024
    for w_ref, p_ref in ((wa_ref, pa_ref), (woz_ref, poz_ref), (wk_ref, pk_ref)):
        for c0 in range(0, w_ref.shape[1], chunk):
            c1 = min(c0 + chunk, w_ref.shape[1])
            p_ref[:, c0:c1] = _dot(h, w_ref[:, c0:c1]).astype(BF16)
    qt_ref[...] = _dot_nt(wqt_ref[...], h).astype(BF16)
    vt_ref[...] = _dot_nt(wvt_ref[...], h).astype(BF16)
    gates_ref[...] = _dot(h, wg_ref[...]) + bg_ref[...]
    gates_t_ref[...] = _dot_nt(wgt_ref[...], h) + bgt_ref[...]


def _even_in(x2, g_pre, w_a, w_oz, w_k, w_qt, w_vt, w_gate, b_gate, w_gate_t, b_gate_t, tm):
    T = x2.shape[0]
    full = lambda a: pl.BlockSpec(a.shape, lambda i: (0, 0))
    row = lambda w: pl.BlockSpec((tm, w), lambda i: (i, 0))
    col = lambda r: pl.BlockSpec((r, tm), lambda i: (0, i))
    weights = (w_a, w_oz, w_k, w_qt, w_vt, w_gate, b_gate, w_gate_t, b_gate_t)
    return pl.pallas_call(
        _even_in_kernel,
        grid=(T // tm,),
        in_specs=[row(D_MODEL), full(g_pre)] + [full(w) for w in weights],
        out_specs=[row(w_a.shape[1]), row(w_oz.shape[1]), row(w_k.shape[1]),
                   col(H_B * DQK_B), col(W_B), row(LANES), col(4 * H_B)],
        out_shape=[
            jax.ShapeDtypeStruct((T, w_a.shape[1]), BF16),
            jax.ShapeDtypeStruct((T, w_oz.shape[1]), BF16),
            jax.ShapeDtypeStruct((T, w_k.shape[1]), BF16),
            jax.ShapeDtypeStruct((H_B * DQK_B, T), BF16),
            jax.ShapeDtypeStruct((W_B, T), BF16),
            jax.ShapeDtypeStruct((T, LANES), F32),
            jax.ShapeDtypeStruct((4 * H_B, T), F32),
        ],
        compiler_params=_params(("arbitrary",)),
        name="even_in_proj",
    )(x2, g_pre, *weights)


CONV_FIRST_TAP = HALO - CONV_K // 2
CONV_SUB = 8
CONV_TILE_TAPS = (CONV_FIRST_TAP + CONV_K - 1) // CONV_SUB + 1


def _conv_shift_matrix(rc):
    span = rc + (CONV_TILE_TAPS - 1) * CONV_SUB
    win = rc + 2 * HALO
    m = np.zeros((CONV_SUB * span, win), np.float32)
    for b in range(CONV_SUB):
        m[b * span + np.arange(span), np.arange(span) + b] = 1.0
    return jnp.asarray(m).astype(BF16)


def _conv_kernel(av_ref, ag_ref, za_ref, avp_ref, agp_ref, avn_ref, agn_ref,
                 cw_ref, cb_ref, gg_ref, gb_ref, sm_ref, out_ref, u_scr, sh_scr, *, ts, rc):
    i = pl.program_id(1)
    last = pl.num_programs(1) - 1

    def gated(a_ref, g_ref):
        return a_ref[...].astype(F32) * _sigmoid(g_ref[...].astype(F32))

    u_scr[HALO:HALO + ts, :] = gated(av_ref, ag_ref).astype(BF16)
    u_scr[0:HALO, :] = jnp.where(i > 0, gated(avp_ref, agp_ref), 0.0).astype(BF16)
    u_scr[HALO + ts:HALO + ts + HALO, :] = jnp.where(i < last, gated(avn_ref, agn_ref), 0.0).astype(BF16)

    span = rc + (CONV_TILE_TAPS - 1) * CONV_SUB
    for r0 in range(0, ts, rc):
        sh_scr[...] = _dot(sm_ref[...], u_scr[r0:r0 + rc + 2 * HALO, :])
        for g in range(A_GROUPS):
            cs = slice(g * LANES, (g + 1) * LANES)
            acc = jnp.zeros((rc, LANES), F32) + cb_ref[:, cs]
            for b in range(CONV_SUB):
                for a in range(CONV_TILE_TAPS):
                    j = CONV_SUB * a + b - CONV_FIRST_TAP
                    if 0 <= j < CONV_K:
                        lo = b * span + CONV_SUB * a
                        acc = acc + sh_scr[lo:lo + rc, cs] * cw_ref[j:j + 1, cs]
            mu = jnp.mean(acc, axis=-1, keepdims=True)
            xc = acc - mu
            var = jnp.mean(xc * xc, axis=-1, keepdims=True)
            y = xc * lax.rsqrt(var + LN_EPS) * gg_ref[:, cs] + gb_ref[:, cs]
            y = _silu(y) * _silu(za_ref[r0:r0 + rc, cs].astype(F32))
            out_ref[r0:r0 + rc, cs] = y.astype(BF16)


def _conv_module(p, conv_w, conv_b, gn_g, gn_b, B, S, ts):
    T = B * S
    nt = S // ts
    hb = ts // HALO
    n_hblk = T // HALO

    def cur(col):
        return pl.BlockSpec((ts, W_A), lambda b, i: (b * nt + i, col))

    def prev(col):
        return pl.BlockSpec((HALO, W_A), lambda b, i: (jnp.maximum((b * nt + i) * hb - 1, 0), col))

    def nxt(col):
        return pl.BlockSpec((HALO, W_A), lambda b, i: (jnp.minimum((b * nt + i + 1) * hb, n_hblk - 1), col))

    def full(r):
        return pl.BlockSpec((r, W_A), lambda b, i: (0, 0))

    rc = min(ts, 128)
    shift = _conv_shift_matrix(rc)
    return pl.pallas_call(
        functools.partial(_conv_kernel, ts=ts, rc=rc),
        grid=(B, nt),
        in_specs=[cur(0), cur(1), cur(2), prev(0), prev(1), nxt(0), nxt(1),
                  full(CONV_K), full(1), full(1), full(1),
                  pl.BlockSpec(shift.shape, lambda b, i: (0, 0))],
        out_specs=pl.BlockSpec((ts, W_A), lambda b, i: (b * nt + i, 0)),
        out_shape=jax.ShapeDtypeStruct((T, W_A), BF16),
        scratch_shapes=[pltpu.VMEM((ts + 2 * HALO, W_A), BF16),
                        pltpu.VMEM((shift.shape[0], W_A), F32)],
        compiler_params=_params(("arbitrary", "arbitrary")),
        name="conv_module",
    )(p, p, p, p, p, p, p, conv_w, conv_b, gn_g, gn_b, shift)


def _mlstm_kernel(qtf_ref, kf_ref, vtf_ref, gcf_ref, grf_ref,
                  qtb_ref, kb_ref, vtb_ref, gcb_ref, grb_ref,
                  hf_ref, hb_ref, st_scr, m_scr, *, L):
    c = pl.program_id(1)

    @pl.when(c == 0)
    def _():
        st_scr[...] = jnp.zeros_like(st_scr)
        m_scr[...] = jnp.zeros_like(m_scr)

    row = lax.broadcasted_iota(jnp.int32, (L, L), 0)
    col = lax.broadcasted_iota(jnp.int32, (L, L), 1)
    lower = col <= row
    upper = col >= row
    lower_f = lower.astype(BF16)
    upper_f = upper.astype(BF16)
    scale = DQK_B ** -0.5
    ones_rows = (lax.broadcasted_iota(jnp.int32, (MLSTM_AUG - DV_B, L), 0) == 0).astype(BF16)

    streams = []
    for fwd, (qt_ref, k_ref, vt_ref, gc_ref, gr_ref, h_ref) in (
            (True, (qtf_ref, kf_ref, vtf_ref, gcf_ref, grf_ref, hf_ref)),
            (False, (qtb_ref, kb_ref, vtb_ref, gcb_ref, grb_ref, hb_ref))):
        gate_i = 0 if fwd else 2 * H_B
        gate_f = gate_i + H_B
        gc = gc_ref[...]
        gr = gr_ref[...]
        b_col_all = _tri_sum_left(lower_f if fwd else upper_f, _log_sigmoid(gc))
        b_row_all = _tri_sum_right(_log_sigmoid(gr), upper_f if fwd else lower_f)
        for h in range(H_B):
            b_col = b_col_all[:, gate_f + h:gate_f + h + 1]
            streams.append(dict(
                fwd=fwd, h=h, sidx=h if fwd else H_B + h, h_ref=h_ref, k_ref=k_ref, vt_ref=vt_ref,
                mask_t=upper if fwd else lower,
                b_col=b_col,
                b_row=b_row_all[gate_f + h:gate_f + h + 1, :],
                i_col=gc[:, gate_i + h:gate_i + h + 1],
                i_row=gr[gate_i + h:gate_i + h + 1, :],
                b_end=b_col[L - 1:L, :] if fwd else b_col[0:1, :],
                qt=(qt_ref[h * DQK_B:(h + 1) * DQK_B, :].astype(F32) * scale).astype(BF16)))

    def keys(s):
        return s["k_ref"][:, s["h"] * DQK_B:(s["h"] + 1) * DQK_B]

    def values(s):
        v = s["vt_ref"][s["h"] * DV_B:(s["h"] + 1) * DV_B, :]
        return jnp.concatenate([v, ones_rows], axis=0)

    for s in streams:
        s["qk"] = _dot(keys(s), s["qt"])
        s["inter"] = _dot(st_scr[s["sidx"]].astype(BF16), s["qt"])
    for s in streams:
        m = m_scr[s["sidx"]][:, 0:1]
        d_t = jnp.where(s["mask_t"], s["b_row"] + (s["i_col"] - s["b_col"]), -jnp.inf)
        a = s["b_row"] + m
        m_t = jnp.maximum(a, jnp.max(d_t, axis=0, keepdims=True))
        s_t = s["qk"] * jnp.exp(d_t - m_t)
        s["num"] = _dot(values(s), s_t.astype(BF16)) + jnp.exp(a - m_t) * s["inter"]
        s["m"], s["m_t"] = m, m_t
    for s in streams:
        den = s["num"][DV_B:DV_B + 1, :]
        hval = s["num"][0:DV_B, :] / jnp.maximum(jnp.abs(den), jnp.exp(-s["m_t"]))
        s["h_ref"][s["h"] * DV_B:(s["h"] + 1) * DV_B, :] = hval.astype(BF16)
    for s in streams:
        g_row = s["b_end"] - s["b_row"] + s["i_row"]
        m_new = jnp.maximum(s["b_end"] + s["m"], jnp.max(g_row, axis=1, keepdims=True))
        wk = jnp.exp(g_row - m_new)
        decay = jnp.exp(s["b_end"] + s["m"] - m_new)
        upd = _dot((values(s).astype(F32) * wk).astype(BF16), keys(s))
        st_scr[s["sidx"]] = decay * st_scr[s["sidx"]] + upd
        m_scr[s["sidx"]] = jnp.broadcast_to(m_new, (1, LANES))


def _mlstm(p, qt, vt, gates, gates_t, B, S, L):
    T = B * S
    nc = S // L

    def specs(chunk):
        return [
            pl.BlockSpec((H_B * DQK_B, L), lambda b, c: (0, b * nc + chunk(c))),
            pl.BlockSpec((L, H_B * DQK_B), lambda b, c: (b * nc + chunk(c), 0)),
            pl.BlockSpec((W_B, L), lambda b, c: (0, b * nc + chunk(c))),
            pl.BlockSpec((L, LANES), lambda b, c: (b * nc + chunk(c), 0)),
            pl.BlockSpec((4 * H_B, L), lambda b, c: (0, b * nc + chunk(c))),
        ]

    fwd = lambda c: c
    bwd = lambda c: nc - 1 - c
    n_streams = 2 * H_B
    return pl.pallas_call(
        functools.partial(_mlstm_kernel, L=L),
        grid=(B, nc),
        in_specs=specs(fwd) + specs(bwd),
        out_specs=[
            pl.BlockSpec((W_B, L), lambda b, c: (0, b * nc + c)),
            pl.BlockSpec((W_B, L), lambda b, c: (0, b * nc + nc - 1 - c)),
        ],
        out_shape=[jax.ShapeDtypeStruct((W_B, T), BF16)] * 2,
        scratch_shapes=[
            pltpu.VMEM((n_streams, MLSTM_AUG, DQK_B), F32),
            pltpu.VMEM((n_streams, 1, LANES), F32),
        ],
        compiler_params=_params(("arbitrary", "arbitrary")),
        name="mlstm",
    )(qt, p, vt, gates, gates_t, qt, p, vt, gates, gates_t)


def _even_out_kernel(ya_ref, hf_ref, hb_ref, o_ref, zb_ref, x_ref, hn_ref, w_ref, g_ref, out_ref):
    parts = []
    for h in range(H_B):
        cs = slice(h * DV_B, (h + 1) * DV_B)
        hm_t = hf_ref[cs, :].astype(F32) + hb_ref[cs, :].astype(F32)
        hm_t = hm_t * lax.rsqrt(jnp.mean(hm_t * hm_t, axis=0, keepdims=True) + RMS_EPS)
        hm = hm_t.T * hn_ref[:, cs]
        yb = _sigmoid(o_ref[:, cs].astype(F32)) * hm * _silu(zb_ref[:, cs].astype(F32))
        parts.append(yb.astype(BF16))
    y = _dot(ya_ref[...], w_ref[0:W_A, :])
    for h in range(H_B):
        y = y + _dot(parts[h], w_ref[W_A + h * DV_B:W_A + (h + 1) * DV_B, :])
    out_ref[...] = x_ref[...] + _rms(y, g_ref[...])


def _even_out(ya, hf, hb, p, x2, hn_g, w_out, g_post, tm):
    T = x2.shape[0]
    row = lambda w: pl.BlockSpec((tm, w), lambda i: (i, 0))
    feat = pl.BlockSpec((W_B, tm), lambda i: (0, i))
    return pl.pallas_call(
        _even_out_kernel,
        grid=(T // tm,),
        in_specs=[row(W_A), feat, feat,
                  pl.BlockSpec((tm, W_B), lambda i: (i, 0)),
                  pl.BlockSpec((tm, W_B), lambda i: (i, 1)),
                  row(D_MODEL),
                  pl.BlockSpec((1, W_B), lambda i: (0, 0)),
                  pl.BlockSpec((W_A + W_B, D_MODEL), lambda i: (0, 0)),
                  pl.BlockSpec((1, D_MODEL), lambda i: (0, 0))],
        out_specs=row(D_MODEL),
        out_shape=jax.ShapeDtypeStruct((T, D_MODEL), F32),
        compiler_params=_params(("arbitrary",)),
        name="even_out_proj",
    )(ya, hf, hb, p, p, x2, hn_g, w_out, g_post)


def _odd_in_kernel(x_ref, g_ref, wq_ref, wkv_ref, wf_ref, wzc_ref, wzd_ref, wkrt_ref,
                   gq_ref, wuqt_ref, gkv_ref, wuk_ref, wuvt_ref, cost_ref, sint_ref,
                   qt_ref, k_ref, vt_ref, f_ref, zc_ref, zd_ref):
    tm = x_ref.shape[0]
    h = _rms(x_ref[...], g_ref[...]).astype(BF16)
    half = QK_ROPE // 2
    ct = cost_ref[...]
    st = sint_ref[...]

    f_all = _dot(h, wf_ref[...]).astype(BF16)
    for g in range(D_GROUPS):
        f_ref[g] = f_all[:, g * D_GROUP_CH:(g + 1) * D_GROUP_CH]
    zc_ref[...] = _dot(h, wzc_ref[...]).astype(BF16)
    zd_ref[...] = _dot(h, wzd_ref[...]).astype(BF16)

    ckv = _rms(_dot(h, wkv_ref[...]), gkv_ref[...]).astype(BF16)
    lane = lax.broadcasted_iota(jnp.int32, (tm, LANES), 1)
    first_half = lane < QK_ROPE
    krt = _dot_nt(wkrt_ref[...], h)
    kr_rot = jnp.concatenate([krt[0:half] * ct - krt[half:QK_ROPE] * st,
                              krt[0:half] * st + krt[half:QK_ROPE] * ct], axis=0)
    kr = jnp.concatenate([kr_rot, kr_rot], axis=0).T
    kr_even = jnp.where(first_half, kr, 0.0).astype(BF16)
    kr_odd = jnp.where(first_half, 0.0, kr).astype(BF16)
    k_all = _dot(ckv, wuk_ref[...]).astype(BF16)
    for hd in range(H_C):
        k_ref[hd, :, 0:QK_NOPE] = k_all[:, hd * QK_NOPE:(hd + 1) * QK_NOPE]
        k_ref[hd, :, QK_NOPE:QK_PAD] = kr_even if hd % 2 == 0 else kr_odd

    vt = _dot_nt(wuvt_ref[...], ckv)
    ones_rows = (lax.broadcasted_iota(jnp.int32, (V_AUG - V_HEAD, tm), 0) == 0).astype(BF16)
    for hd in range(H_C):
        vt_ref[hd, 0:V_HEAD, :] = vt[hd * V_HEAD:(hd + 1) * V_HEAD, :].astype(BF16)
        vt_ref[hd, V_HEAD:V_AUG, :] = ones_rows

    scale = (QK_NOPE + QK_ROPE) ** -0.5 * math.log2(math.e)
    cq = _rms(_dot(h, wq_ref[...]), gq_ref[...]).astype(BF16)
    qn = _dot_nt(wuqt_ref[0:H_C * QK_NOPE, :], cq) * scale
    qr = _dot_nt(wuqt_ref[H_C * QK_NOPE:H_C * (QK_NOPE + QK_ROPE), :], cq) * scale
    zeros = jnp.zeros((QK_ROPE, tm), BF16)
    for hd in range(H_C):
        qt_ref[hd, 0:QK_NOPE, :] = qn[hd * QK_NOPE:(hd + 1) * QK_NOPE, :].astype(BF16)
        x1 = qr[hd * QK_ROPE:hd * QK_ROPE + half, :]
        x2 = qr[hd * QK_ROPE + half:(hd + 1) * QK_ROPE, :]
        lo = QK_NOPE if hd % 2 == 0 else QK_NOPE + QK_ROPE
        pad = QK_NOPE + QK_ROPE if hd % 2 == 0 else QK_NOPE
        qt_ref[hd, lo:lo + half, :] = (x1 * ct - x2 * st).astype(BF16)
        qt_ref[hd, lo + half:lo + QK_ROPE, :] = (x1 * st + x2 * ct).astype(BF16)
        qt_ref[hd, pad:pad + QK_ROPE, :] = zeros


def _odd_in(x2, g_pre, w_segs, g_q, w_uq_t, g_kv, w_uk, w_uv_t, cos_t, sin_t, B, S, tm):
    T = B * S
    nt = S // tm
    full = lambda a: pl.BlockSpec(a.shape, lambda i: (0,) * a.ndim)
    row = lambda w: pl.BlockSpec((tm, w), lambda i: (i, 0))
    return pl.pallas_call(
        _odd_in_kernel,
        grid=(T // tm,),
        in_specs=[row(D_MODEL), full(g_pre)] + [full(w) for w in w_segs]
                 + [full(g_q), full(w_uq_t), full(g_kv), full(w_uk), full(w_uv_t),
                    pl.BlockSpec((QK_ROPE // 2, tm), lambda i: (0, i)),
                    pl.BlockSpec((QK_ROPE // 2, tm), lambda i: (0, i))],
        out_specs=[pl.BlockSpec((None, H_C, QK_PAD, tm), lambda i: (i // nt, 0, 0, i % nt)),
                   pl.BlockSpec((None, H_C, tm, QK_PAD), lambda i: (i // nt, 0, i % nt, 0)),
                   pl.BlockSpec((None, H_C, None, V_AUG, tm), lambda i: (i // nt, 0, i % nt, 0, 0)),
                   pl.BlockSpec((D_GROUPS, None, tm, D_GROUP_CH), lambda i: (0, i // nt, i % nt, 0)),
                   row(W_C), row(W_D)],
        out_shape=[
            jax.ShapeDtypeStruct((B, H_C, QK_PAD, S), BF16),
            jax.ShapeDtypeStruct((B, H_C, S, QK_PAD), BF16),
            jax.ShapeDtypeStruct((B, H_C, nt, V_AUG, tm), BF16),
            jax.ShapeDtypeStruct((D_GROUPS, B, S, D_GROUP_CH), BF16),
            jax.ShapeDtypeStruct((T, W_C), BF16),
            jax.ShapeDtypeStruct((T, W_D), BF16),
        ],
        compiler_params=_params(("arbitrary",)),
        name="odd_in_proj",
    )(x2, g_pre, *w_segs, g_q, w_uq_t, g_kv, w_uk, w_uv_t, cos_t, sin_t)


def _attn_kernel(qt_ref, k_ref, vt_ref, o_ref, s_a, s_b, p_a, p_b, acc_scr, *, tk):
    nk = k_ref.shape[0] // tk
    qt = qt_ref[...]

    s_bufs = (s_a, s_b)
    p_bufs = (p_a, p_b)

    def scores(j):
        return _dot(k_ref[j * tk:(j + 1) * tk, :], qt)

    tv = vt_ref.shape[-1]

    def v_tile(j):
        return vt_ref[j * tk // tv][:, j * tk % tv:j * tk % tv + tk]

    s_bufs[0][...] = scores(0)
    m = None
    alpha_prev = None
    for j in range(nk):
        if j + 1 < nk:
            s_bufs[(j + 1) % 2][...] = scores(j + 1)
        s = s_bufs[j % 2][...]
        tile_max = jnp.max(s, axis=0, keepdims=True)
        m_new = tile_max if m is None else jnp.maximum(m, tile_max)
        if j >= 1:
            pv = _dot(v_tile(j - 1), p_bufs[(j - 1) % 2][...])
            acc_scr[...] = pv if j == 1 else alpha_prev * acc_scr[...] + pv
        alpha_prev = None if m is None else jnp.exp2(m - m_new)
        p_bufs[j % 2][...] = jnp.exp2((s - m_new).astype(BF16))
        m = m_new
    acc = alpha_prev * acc_scr[...] + _dot(v_tile(nk - 1), p_bufs[(nk - 1) % 2][...])
    o_ref[...] = (acc[0:V_HEAD, :] / acc[V_HEAD:V_HEAD + 1, :]).T.astype(BF16)


def _attention(qt, k, vt, B, S, tq, tk):
    nv, tv = vt.shape[2], vt.shape[4]
    assert S // tk >= 2 and tv % tk == 0
    return pl.pallas_call(
        functools.partial(_attn_kernel, tk=tk),
        grid=(B, H_C, S // tq),
        in_specs=[
            pl.BlockSpec((None, None, QK_PAD, tq), lambda b, h, i: (b, h, 0, i)),
            pl.BlockSpec((None, None, S, QK_PAD), lambda b, h, i: (b, h, 0, 0)),
            pl.BlockSpec((None, None, nv, V_AUG, tv), lambda b, h, i: (b, h, 0, 0, 0)),
        ],
        out_specs=pl.BlockSpec((None, tq, V_HEAD), lambda b, h, i: (b, i, h)),
        out_shape=jax.ShapeDtypeStruct((B, S, W_C), BF16),
        scratch_shapes=[pltpu.VMEM((tk, tq), F32), pltpu.VMEM((tk, tq), F32),
                        pltpu.VMEM((tk, tq), BF16), pltpu.VMEM((tk, tq), BF16),
                        pltpu.VMEM((V_AUG, tq), F32)],
        compiler_params=_params(("arbitrary", "arbitrary", "arbitrary")),
        name="flash_attention",
    )(qt, k, vt)


def _fft_tables(S):
    n1 = S // LANES
    a = np.arange(n1, dtype=np.float64)
    ang1 = 2.0 * np.pi * np.outer(a, a) / n1
    w1 = np.concatenate([np.cos(ang1), -np.sin(ang1)], axis=0)
    k1 = np.arange(n1)[:, None, None]
    k2 = np.arange(LANES)[None, :, None]
    n2 = np.arange(LANES)[None, None, :]
    kk = (k1 + n1 * k2) * n2 % S
    ang2 = 2.0 * np.pi * kk.astype(np.float64) / S
    gc = np.cos(ang2).reshape(n1 * LANES, LANES)
    gs = np.sin(ang2).reshape(n1 * LANES, LANES)
    c = np.arange(D_GROUP_CH, dtype=np.float64)
    angc = 2.0 * np.pi * np.outer(c, c) / D_GROUP_CH
    f32 = lambda t: jnp.asarray(t.astype(np.float32))
    return f32(w1), f32(gc), f32(gs), f32(np.cos(angc)), f32(np.sin(angc))


def _fft_pitch(n1):
    return n1 + 8


def _fft_kernel(x_ref, w1_ref, gc_ref, gs_ref, cc_ref, sc_ref, out_ref, ar_scr, ai_scr, z_scr, *, n1, cw):
    n_chunks = (LANES * LANES) // cw
    per = cw // LANES
    pitch = _fft_pitch(n1)
    w1 = w1_ref[...]

    def stage1(ch, carry):
        lo = pl.multiple_of(ch * cw, cw)
        res = _dot(w1, x_ref[:, pl.ds(lo, cw)])
        for j in range(per):
            r0 = pl.multiple_of((ch * per + j) * pitch, 8) if pitch % 8 == 0 else (ch * per + j) * pitch
            ar_scr[pl.ds(r0, n1), :] = res[0:n1, j * LANES:(j + 1) * LANES]
            ai_scr[pl.ds(r0, n1), :] = res[n1:2 * n1, j * LANES:(j + 1) * LANES]
        return carry

    lax.fori_loop(0, n_chunks, stage1, 0)

    ccm = cc_ref[...]
    scm = sc_ref[...]
    norm = 1.0 / math.sqrt(n1 * LANES * D_GROUP_CH)

    def sequence_dft(k1):
        ar = ar_scr[pl.ds(k1, LANES, stride=pitch), :]
        ai = ai_scr[pl.ds(k1, LANES, stride=pitch), :]
        a = jnp.concatenate([ar, ai], axis=1).astype(BF16)
        t0 = pl.multiple_of(k1 * LANES, LANES)
        p1 = _dot(gc_ref[pl.ds(t0, LANES), :], a)
        p2 = _dot(gs_ref[pl.ds(t0, LANES), :], a)
        zr = p1[:, 0:LANES] + p2[:, LANES:2 * LANES]
        zi = p1[:, LANES:2 * LANES] - p2[:, 0:LANES]
        return zr.astype(BF16), zi.astype(BF16)

    def channel_dft(k1, z):
        fr = _dot(z[0], ccm) + _dot(z[1], scm)
        z_scr[pl.ds(k1, LANES, stride=pitch), :] = fr * norm

    group = 8 if n1 % 8 == 0 else 1

    def stage2(i, carry):
        zs = [sequence_dft(i * group + u) for u in range(group)]
        for u in range(group):
            channel_dft(i * group + u, zs[u])
        return carry

    lax.fori_loop(0, n1 // group, stage2, 0)

    for k2 in range(LANES):
        out_ref[k2 * n1:(k2 + 1) * n1, :] = z_scr[k2 * pitch:k2 * pitch + n1, :]


def _fft(f4, B, S):
    n1 = S // LANES
    w1, gc, gs, cc, sc = _fft_tables(S)
    bf = lambda t: t.astype(BF16)
    x4 = f4.reshape(D_GROUPS, B, n1, LANES * D_GROUP_CH)
    cw = 2048
    full = lambda a: pl.BlockSpec(a.shape, lambda g, b: (0,) * a.ndim)
    tabs = [bf(w1), bf(gc), bf(gs), bf(cc), bf(sc)]
    return pl.pallas_call(
        functools.partial(_fft_kernel, n1=n1, cw=cw),
        grid=(D_GROUPS, B),
        in_specs=[pl.BlockSpec((None, None, n1, LANES * D_GROUP_CH), lambda g, b: (g, b, 0, 0))]
                 + [full(t) for t in tabs],
        out_specs=pl.BlockSpec((None, S, D_GROUP_CH), lambda g, b: (b, 0, g)),
        out_shape=jax.ShapeDtypeStruct((B, S, W_D), F32),
        scratch_shapes=[pltpu.VMEM((LANES * _fft_pitch(n1), D_GROUP_CH), F32)] * 3,
        compiler_params=_params(("arbitrary", "arbitrary")),
        name="fft2_real",
    )(x4, *tabs)


def _odd_out_kernel(att_ref, zc_ref, fr_ref, zd_ref, x_ref, wfd_ref, w_ref, g_ref, out_ref):
    yc = (att_ref[...].astype(F32) * _silu(zc_ref[...].astype(F32))).astype(BF16)
    yd = (_dot(fr_ref[...].astype(BF16), wfd_ref[...]) * _silu(zd_ref[...].astype(F32))).astype(BF16)
    y = _dot(yc, w_ref[0:W_C, :]) + _dot(yd, w_ref[W_C:W_C + W_D, :])
    out_ref[...] = x_ref[...] + _rms(y, g_ref[...])


def _odd_out(att, zc, fr, zd, x2, w_fd, w_out, g_post, tm):
    T = x2.shape[0]
    row = lambda w: pl.BlockSpec((tm, w), lambda i: (i, 0))
    full = lambda a: pl.BlockSpec(a.shape, lambda i: (0,) * a.ndim)
    return pl.pallas_call(
        _odd_out_kernel,
        grid=(T // tm,),
        in_specs=[row(W_C), row(W_C), row(W_D), row(W_D), row(D_MODEL), full(w_fd), full(w_out), full(g_post)],
        out_specs=row(D_MODEL),
        out_shape=jax.ShapeDtypeStruct((T, D_MODEL), F32),
        compiler_params=_params(("arbitrary",)),
        name="odd_out_proj",
    )(att, zc, fr, zd, x2, w_fd, w_out, g_post)


def _even_layer(x2, B, S, g_pre, g_post, w_in, b_gate, conv_w, conv_b, gn_g, gn_b, hn_g, w_out):
    o = np.cumsum([0, 3 * W_A, H_B * DQK_B, H_B * DQK_B, W_B, 2 * W_B, 4 * H_B])
    seg = lambda i: w_in[:, o[i]:o[i + 1]].astype(BF16)
    w_gate = jnp.pad(seg(5), ((0, 0), (0, LANES - 4 * H_B)))
    bg = jnp.pad(b_gate, (0, LANES - 4 * H_B)).reshape(1, LANES)
    tm = min(512, S)
    pa, poz, pk, qt, vt, gates, gates_t = _even_in(
        x2, g_pre.reshape(1, -1), seg(0), seg(4), seg(2), seg(1).T, seg(3).T, w_gate, bg,
        seg(5).T, b_gate.reshape(-1, 1), tm)
    ya = _conv_module(pa, conv_w, conv_b.reshape(1, -1), gn_g.reshape(1, -1), gn_b.reshape(1, -1),
                      B, S, min(256, S))
    hf, hb = _mlstm(pk, qt, vt, gates, gates_t, B, S, min(MLSTM_CHUNK, S))
    return _even_out(ya, hf, hb, poz, x2, hn_g.reshape(1, -1), w_out.astype(BF16), g_post.reshape(1, -1),
                     min(512, S))


def _odd_layer(x2, B, S, cos, sin, g_pre, g_post, w_in, g_q, w_uq, g_kv, w_ukv, w_fd, w_out):
    T = B * S
    o = np.cumsum([0, Q_LORA, KV_LORA, QK_ROPE, W_D, W_C, W_D])
    seg = lambda i: w_in[:, o[i]:o[i + 1]].astype(BF16)
    w_segs = [seg(0), seg(1), seg(3), seg(4), seg(5), seg(2).T]
    wq = w_uq.reshape(Q_LORA, H_C, QK_NOPE + QK_ROPE)
    w_uq_t = jnp.concatenate([wq[:, :, :QK_NOPE].reshape(Q_LORA, -1),
                              wq[:, :, QK_NOPE:].reshape(Q_LORA, -1)], axis=1).T.astype(BF16)
    wkv = w_ukv.reshape(KV_LORA, H_C, QK_NOPE + V_HEAD)
    w_uk = wkv[:, :, :QK_NOPE].reshape(KV_LORA, -1).astype(BF16)
    w_uv_t = wkv[:, :, QK_NOPE:].reshape(KV_LORA, -1).T.astype(BF16)
    cos_t = cos.reshape(T, QK_ROPE // 2).T
    sin_t = sin.reshape(T, QK_ROPE // 2).T
    tm = min(512, S // 2)
    qt, k, vt, f4, zc, zd = _odd_in(x2, g_pre.reshape(1, -1), w_segs, g_q.reshape(1, -1), w_uq_t,
                                    g_kv.reshape(1, -1), w_uk, w_uv_t, cos_t, sin_t, B, S, tm)
    att = _attention(qt, k, vt, B, S, min(ATTN_TQ, S), min(ATTN_TK, tm)).reshape(T, W_C)
    fr = _fft(f4, B, S).reshape(T, W_D)
    return _odd_out(att, zc, fr, zd, x2, w_fd.astype(BF16), w_out.astype(BF16), g_post.reshape(1, -1), tm)


def _rope_tables(positions):
    inv = ROPE_THETA ** (-jnp.arange(0, QK_ROPE, 2, dtype=F32) / QK_ROPE)
    ang = positions.astype(F32)[..., None] * inv
    return jnp.cos(ang), jnp.sin(ang)


def kernel(x, positions, even_g_pre, even_g_post, even_w_in, even_b_gate, even_conv_w, even_conv_b,
           even_gn_g, even_gn_b, even_hn_g, even_w_out, odd_g_pre, odd_g_post, odd_w_in, odd_g_q,
           odd_w_uq, odd_g_kv, odd_w_ukv, odd_w_fd, odd_w_out):
    B, S, _ = x.shape
    cos, sin = _rope_tables(positions)
    h = x.reshape(B * S, D_MODEL)
    depth = even_w_in.shape[0] + odd_w_in.shape[0]
    for layer in range(depth):
        j = layer // 2
        if layer % 2 == 0:
            h = _even_layer(h, B, S, even_g_pre[j], even_g_post[j], even_w_in[j], even_b_gate[j],
                            even_conv_w[j], even_conv_b[j], even_gn_g[j], even_gn_b[j], even_hn_g[j],
                            even_w_out[j])
        else:
            h = _odd_layer(h, B, S, cos, sin, odd_g_pre[j], odd_g_post[j], odd_w_in[j], odd_g_q[j],
                           odd_w_uq[j], odd_g_kv[j], odd_w_ukv[j], odd_w_fd[j], odd_w_out[j])
    return h.reshape(B, S, D_MODEL)
```

```python
import functools
import math

import numpy as np
import jax
import jax.numpy as jnp
from jax import lax
from jax.experimental import pallas as pl
from jax.experimental.pallas import tpu as pltpu

F32 = jnp.float32
BF16 = jnp.bfloat16

D_MODEL = 1024
RMS_EPS = 1e-6
LN_EPS = 1e-5
W_A = 1024
CONV_K = 31
A_GROUPS = 8
H_B = 4
DQK_B = 128
DV_B = 256
W_B = 1024
H_C = 8
QK_NOPE = 128
QK_ROPE = 64
V_HEAD = 128
Q_LORA = 384
KV_LORA = 256
W_C = 1024
ROPE_THETA = 10000.0
D_GROUPS = 4
D_GROUP_CH = 128
W_D = 512

LANES = 128
HALO = 16
QK_PAD = 256
V_AUG = V_HEAD + 16
VMEM_LIMIT = 56 * 1024 * 1024

MLSTM_CHUNK = 256
MLSTM_AUG = DV_B + 16
ATTN_TQ = 1024
ATTN_TK = 512
ATTN_TS = 2048
NEG_BIG = -1e30


def _params(sem, vmem=VMEM_LIMIT):
    return pltpu.CompilerParams(dimension_semantics=sem, vmem_limit_bytes=vmem)


def _sigmoid(x):
    return 1.0 / (1.0 + jnp.exp(-x))


def _silu(x):
    return x * _sigmoid(x)


def _log_sigmoid(x):
    return jnp.minimum(x, 0.0) - jnp.log(1.0 + jnp.exp(-jnp.abs(x)))


def _rms(x, g):
    return x * lax.rsqrt(jnp.mean(x * x, axis=-1, keepdims=True) + RMS_EPS) * g


def _dot(a, b):
    return jnp.dot(a, b, preferred_element_type=F32)


def _dot_nt(a, b):
    return lax.dot_general(a, b, (((1,), (1,)), ((), ())), preferred_element_type=F32)


def _split3(x):
    hi = x.astype(BF16)
    r1 = x - hi.astype(F32)
    mid = r1.astype(BF16)
    lo = (r1 - mid.astype(F32)).astype(BF16)
    return hi, mid, lo


def _tri_sum_left(tri, x):
    hi, mid, lo = _split3(x)
    return _dot(tri, lo) + _dot(tri, mid) + _dot(tri, hi)


def _tri_sum_right(x, tri):
    hi, mid, lo = _split3(x)
    return _dot(lo, tri) + _dot(mid, tri) + _dot(hi, tri)


def _even_in_kernel(x_ref, g_ref, wa_ref, woz_ref, wk_ref, wqt_ref, wvt_ref, wg_ref, bg_ref, wgt_ref, bgt_ref,
                    pa_ref, poz_ref, pk_ref, qt_ref, vt_ref, gates_ref, gates_t_ref):
    h = _rms(x_ref[...], g_ref[...]).astype(BF16)
    chunk = 1024
    for w_ref, p_ref in ((wa_ref, pa_ref), (woz_ref, poz_ref), (wk_ref, pk_ref)):
        for c0 in range(0, w_ref.shape[1], chunk):
            c1 = min(c0 + chunk, w_ref.shape[1])
            p_ref[:, c0:c1] = _dot(h, w_ref[:, c0:c1]).astype(BF16)
    qt_ref[...] = _dot_nt(wqt_ref[...], h).astype(BF16)
    vt_ref[...] = _dot_nt(wvt_ref[...], h).astype(BF16)
    gates_ref[...] = _dot(h, wg_ref[...]) + bg_ref[...]
    gates_t_ref[...] = _dot_nt(wgt_ref[...], h) + bgt_ref[...]


def _even_in(x2, g_pre, w_a, w_oz, w_k, w_qt, w_vt, w_gate, b_gate, w_gate_t, b_gate_t, tm):
    T = x2.shape[0]
    full = lambda a: pl.BlockSpec(a.shape, lambda i: (0, 0))
    row = lambda w: pl.BlockSpec((tm, w), lambda i: (i, 0))
    col = lambda r: pl.BlockSpec((r, tm), lambda i: (0, i))
    weights = (w_a, w_oz, w_k, w_qt, w_vt, w_gate, b_gate, w_gate_t, b_gate_t)
    return pl.pallas_call(
        _even_in_kernel,
        grid=(T // tm,),
        in_specs=[row(D_MODEL), full(g_pre)] + [full(w) for w in weights],
        out_specs=[row(w_a.shape[1]), row(w_oz.shape[1]), row(w_k.shape[1]),
                   col(H_B * DQK_B), col(W_B), row(LANES), col(4 * H_B)],
        out_shape=[
            jax.ShapeDtypeStruct((T, w_a.shape[1]), BF16),
            jax.ShapeDtypeStruct((T, w_oz.shape[1]), BF16),
            jax.ShapeDtypeStruct((T, w_k.shape[1]), BF16),
            jax.ShapeDtypeStruct((H_B * DQK_B, T), BF16),
            jax.ShapeDtypeStruct((W_B, T), BF16),
            jax.ShapeDtypeStruct((T, LANES), F32),
            jax.ShapeDtypeStruct((4 * H_B, T), F32),
        ],
        compiler_params=_params(("arbitrary",)),
        name="even_in_proj",
    )(x2, g_pre, *weights)


CONV_FIRST_TAP = HALO - CONV_K // 2
CONV_SUB = 8
CONV_TILE_TAPS = (CONV_FIRST_TAP + CONV_K - 1) // CONV_SUB + 1


def _conv_shift_matrix(rc):
    span = rc + (CONV_TILE_TAPS - 1) * CONV_SUB
    win = rc + 2 * HALO
    m = np.zeros((CONV_SUB * span, win), np.float32)
    for b in range(CONV_SUB):
        m[b * span + np.arange(span), np.arange(span) + b] = 1.0
    return jnp.asarray(m).astype(BF16)


def _conv_kernel(av_ref, ag_ref, za_ref, avp_ref, agp_ref, avn_ref, agn_ref,
                 cw_ref, cb_ref, gg_ref, gb_ref, sm_ref, out_ref, u_scr, sh_scr, *, ts, rc):
    i = pl.program_id(1)
    last = pl.num_programs(1) - 1

    def gated(a_ref, g_ref):
        return a_ref[...].astype(F32) * _sigmoid(g_ref[...].astype(F32))

    u_scr[HALO:HALO + ts, :] = gated(av_ref, ag_ref).astype(BF16)
    u_scr[0:HALO, :] = jnp.where(i > 0, gated(avp_ref, agp_ref), 0.0).astype(BF16)
    u_scr[HALO + ts:HALO + ts + HALO, :] = jnp.where(i < last, gated(avn_ref, agn_ref), 0.0).astype(BF16)

    span = rc + (CONV_TILE_TAPS - 1) * CONV_SUB
    for r0 in range(0, ts, rc):
        sh_scr[...] = _dot(sm_ref[...], u_scr[r0:r0 + rc + 2 * HALO, :])
        for g in range(A_GROUPS):
            cs = slice(g * LANES, (g + 1) * LANES)
            acc = jnp.zeros((rc, LANES), F32) + cb_ref[:, cs]
            for b in range(CONV_SUB):
                for a in range(CONV_TILE_TAPS):
                    j = CONV_SUB * a + b - CONV_FIRST_TAP
                    if 0 <= j < CONV_K:
                        lo = b * span + CONV_SUB * a
                        acc = acc + sh_scr[lo:lo + rc, cs] * cw_ref[j:j + 1, cs]
            mu = jnp.mean(acc, axis=-1, keepdims=True)
            xc = acc - mu
            var = jnp.mean(xc * xc, axis=-1, keepdims=True)
            y = xc * lax.rsqrt(var + LN_EPS) * gg_ref[:, cs] + gb_ref[:, cs]
            y = _silu(y) * _silu(za_ref[r0:r0 + rc, cs].astype(F32))
            out_ref[r0:r0 + rc, cs] = y.astype(BF16)


def _conv_module(p, conv_w, conv_b, gn_g, gn_b, B, S, ts):
    T = B * S
    nt = S // ts
    hb = ts // HALO
    n_hblk = T // HALO

    def cur(col):
        return pl.BlockSpec((ts, W_A), lambda b, i: (b * nt + i, col))

    def prev(col):
        return pl.BlockSpec((HALO, W_A), lambda b, i: (jnp.maximum((b * nt + i) * hb - 1, 0), col))

    def nxt(col):
        return pl.BlockSpec((HALO, W_A), lambda b, i: (jnp.minimum((b * nt + i + 1) * hb, n_hblk - 1), col))

    def full(r):
        return pl.BlockSpec((r, W_A), lambda b, i: (0, 0))

    rc = min(ts, 128)
    shift = _conv_shift_matrix(rc)
    return pl.pallas_call(
        functools.partial(_conv_kernel, ts=ts, rc=rc),
        grid=(B, nt),
        in_specs=[cur(0), cur(1), cur(2), prev(0), prev(1), nxt(0), nxt(1),
                  full(CONV_K), full(1), full(1), full(1),
                  pl.BlockSpec(shift.shape, lambda b, i: (0, 0))],
        out_specs=pl.BlockSpec((ts, W_A), lambda b, i: (b * nt + i, 0)),
        out_shape=jax.ShapeDtypeStruct((T, W_A), BF16),
        scratch_shapes=[pltpu.VMEM((ts + 2 * HALO, W_A), BF16),
                        pltpu.VMEM((shift.shape[0], W_A), F32)],
        compiler_params=_params(("arbitrary", "arbitrary")),
        name="conv_module",
    )(p, p, p, p, p, p, p, conv_w, conv_b, gn_g, gn_b, shift)


def _mlstm_kernel(qtf_ref, kf_ref, vtf_ref, gcf_ref, grf_ref,
                  qtb_ref, kb_ref, vtb_ref, gcb_ref, grb_ref,
                  hf_ref, hb_ref, st_scr, m_scr, *, L):
    c = pl.program_id(1)

    @pl.when(c == 0)
    def _():
        st_scr[...] = jnp.zeros_like(st_scr)
        m_scr[...] = jnp.zeros_like(m_scr)

    row = lax.broadcasted_iota(jnp.int32, (L, L), 0)
    col = lax.broadcasted_iota(jnp.int32, (L, L), 1)
    lower = col <= row
    upper = col >= row
    lower_f = lower.astype(BF16)
    upper_f = upper.astype(BF16)
    scale = DQK_B ** -0.5
    ones_rows = (lax.broadcasted_iota(jnp.int32, (MLSTM_AUG - DV_B, L), 0) == 0).astype(BF16)

    streams = []
    for fwd, (qt_ref, k_ref, vt_ref, gc_ref, gr_ref, h_ref) in (
            (True, (qtf_ref, kf_ref, vtf_ref, gcf_ref, grf_ref, hf_ref)),
            (False, (qtb_ref, kb_ref, vtb_ref, gcb_ref, grb_ref, hb_ref))):
        gate_i = 0 if fwd else 2 * H_B
        gate_f = gate_i + H_B
        gc = gc_ref[...]
        gr = gr_ref[...]
        b_col_all = _tri_sum_left(lower_f if fwd else upper_f, _log_sigmoid(gc))
        b_row_all = _tri_sum_right(_log_sigmoid(gr), upper_f if fwd else lower_f)
        for h in range(H_B):
            b_col = b_col_all[:, gate_f + h:gate_f + h + 1]
            streams.append(dict(
                fwd=fwd, h=h, sidx=h if fwd else H_B + h, h_ref=h_ref, k_ref=k_ref, vt_ref=vt_ref,
                mask_t=upper if fwd else lower,
                b_col=b_col,
                b_row=b_row_all[gate_f + h:gate_f + h + 1, :],
                i_col=gc[:, gate_i + h:gate_i + h + 1],
                i_row=gr[gate_i + h:gate_i + h + 1, :],
                b_end=b_col[L - 1:L, :] if fwd else b_col[0:1, :],
                qt=(qt_ref[h * DQK_B:(h + 1) * DQK_B, :].astype(F32) * scale).astype(BF16)))

    def keys(s):
        return s["k_ref"][:, s["h"] * DQK_B:(s["h"] + 1) * DQK_B]

    def values(s):
        v = s["vt_ref"][s["h"] * DV_B:(s["h"] + 1) * DV_B, :]
        return jnp.concatenate([v, ones_rows], axis=0)

    for s in streams:
        s["qk"] = _dot(keys(s), s["qt"])
        s["inter"] = _dot(st_scr[s["sidx"]].astype(BF16), s["qt"])
    for s in streams:
        m = m_scr[s["sidx"]][:, 0:1]
        d_t = jnp.where(s["mask_t"], s["b_row"] + (s["i_col"] - s["b_col"]), -jnp.inf)
        a = s["b_row"] + m
        m_t = jnp.maximum(a, jnp.max(d_t, axis=0, keepdims=True))
        s_t = s["qk"] * jnp.exp(d_t - m_t)
        s["num"] = _dot(values(s), s_t.astype(BF16)) + jnp.exp(a - m_t) * s["inter"]
        s["m"], s["m_t"] = m, m_t
    for s in streams:
        den = s["num"][DV_B:DV_B + 1, :]
        hval = s["num"][0:DV_B, :] / jnp.maximum(jnp.abs(den), jnp.exp(-s["m_t"]))
        s["h_ref"][s["h"] * DV_B:(s["h"] + 1) * DV_B, :] = hval.astype(BF16)
    for s in streams:
        g_row = s["b_end"] - s["b_row"] + s["i_row"]
        m_new = jnp.maximum(s["b_end"] + s["m"], jnp.max(g_row, axis=1, keepdims=True))
        wk = jnp.exp(g_row - m_new)
        decay = jnp.exp(s["b_end"] + s["m"] - m_new)
        upd = _dot((values(s).astype(F32) * wk).astype(BF16), keys(s))
        st_scr[s["sidx"]] = decay * st_scr[s["sidx"]] + upd
        m_scr[s["sidx"]] = jnp.broadcast_to(m_new, (1, LANES))


def _mlstm(p, qt, vt, gates, gates_t, B, S, L):
    T = B * S
    nc = S // L

    def specs(chunk):
        return [
            pl.BlockSpec((H_B * DQK_B, L), lambda b, c: (0, b * nc + chunk(c))),
            pl.BlockSpec((L, H_B * DQK_B), lambda b, c: (b * nc + chunk(c), 0)),
            pl.BlockSpec((W_B, L), lambda b, c: (0, b * nc + chunk(c))),
            pl.BlockSpec((L, LANES), lambda b, c: (b * nc + chunk(c), 0)),
            pl.BlockSpec((4 * H_B, L), lambda b, c: (0, b * nc + chunk(c))),
        ]

    fwd = lambda c: c
    bwd = lambda c: nc - 1 - c
    n_streams = 2 * H_B
    return pl.pallas_call(
        functools.partial(_mlstm_kernel, L=L),
        grid=(B, nc),
        in_specs=specs(fwd) + specs(bwd),
        out_specs=[
            pl.BlockSpec((W_B, L), lambda b, c: (0, b * nc + c)),
            pl.BlockSpec((W_B, L), lambda b, c: (0, b * nc + nc - 1 - c)),
        ],
        out_shape=[jax.ShapeDtypeStruct((W_B, T), BF16)] * 2,
        scratch_shapes=[
            pltpu.VMEM((n_streams, MLSTM_AUG, DQK_B), F32),
            pltpu.VMEM((n_streams, 1, LANES), F32),
        ],
        compiler_params=_params(("arbitrary", "arbitrary")),
        name="mlstm",
    )(qt, p, vt, gates, gates_t, qt, p, vt, gates, gates_t)


def _even_out_kernel(ya_ref, hf_ref, hb_ref, o_ref, zb_ref, x_ref, hn_ref, w_ref, g_ref, out_ref):
    parts = []
    for h in range(H_B):
        cs = slice(h * DV_B, (h + 1) * DV_B)
        hm_t = hf_ref[cs, :].astype(F32) + hb_ref[cs, :].astype(F32)
        hm_t = hm_t * lax.rsqrt(jnp.mean(hm_t * hm_t, axis=0, keepdims=True) + RMS_EPS)
        hm = hm_t.T * hn_ref[:, cs]
        yb = _sigmoid(o_ref[:, cs].astype(F32)) * hm * _silu(zb_ref[:, cs].astype(F32))
        parts.append(yb.astype(BF16))
    y = _dot(ya_ref[...], w_ref[0:W_A, :])
    for h in range(H_B):
        y = y + _dot(parts[h], w_ref[W_A + h * DV_B:W_A + (h + 1) * DV_B, :])
    out_ref[...] = x_ref[...] + _rms(y, g_ref[...])


def _even_out(ya, hf, hb, p, x2, hn_g, w_out, g_post, tm):
    T = x2.shape[0]
    row = lambda w: pl.BlockSpec((tm, w), lambda i: (i, 0))
    feat = pl.BlockSpec((W_B, tm), lambda i: (0, i))
    return pl.pallas_call(
        _even_out_kernel,
        grid=(T // tm,),
        in_specs=[row(W_A), feat, feat,
                  pl.BlockSpec((tm, W_B), lambda i: (i, 0)),
                  pl.BlockSpec((tm, W_B), lambda i: (i, 1)),
                  row(D_MODEL),
                  pl.BlockSpec((1, W_B), lambda i: (0, 0)),
                  pl.BlockSpec((W_A + W_B, D_MODEL), lambda i: (0, 0)),
                  pl.BlockSpec((1, D_MODEL), lambda i: (0, 0))],
        out_specs=row(D_MODEL),
        out_shape=jax.ShapeDtypeStruct((T, D_MODEL), F32),
        compiler_params=_params(("arbitrary",)),
        name="even_out_proj",
    )(ya, hf, hb, p, p, x2, hn_g, w_out, g_post)


def _odd_in_kernel(x_ref, g_ref, wq_ref, wkv_ref, wf_ref, wzc_ref, wzd_ref, wkrt_ref,
                   gq_ref, wuqt_ref, gkv_ref, wuk_ref, wuvt_ref, cost_ref, sint_ref,
                   qt_ref, k_ref, vt_ref, f_ref, zc_ref, zd_ref):
    tm = x_ref.shape[0]
    h = _rms(x_ref[...], g_ref[...]).astype(BF16)
    half = QK_ROPE // 2
    ct = cost_ref[...]
    st = sint_ref[...]

    f_all = _dot(h, wf_ref[...]).astype(BF16)
    for g in range(D_GROUPS):
        f_ref[g] = f_all[:, g * D_GROUP_CH:(g + 1) * D_GROUP_CH]
    zc_ref[...] = _dot(h, wzc_ref[...]).astype(BF16)
    zd_ref[...] = _dot(h, wzd_ref[...]).astype(BF16)

    ckv = _rms(_dot(h, wkv_ref[...]), gkv_ref[...]).astype(BF16)
    lane = lax.broadcasted_iota(jnp.int32, (tm, LANES), 1)
    first_half = lane < QK_ROPE
    krt = _dot_nt(wkrt_ref[...], h)
    kr_rot = jnp.concatenate([krt[0:half] * ct - krt[half:QK_ROPE] * st,
                              krt[0:half] * st + krt[half:QK_ROPE] * ct], axis=0)
    kr = jnp.concatenate([kr_rot, kr_rot], axis=0).T
    kr_even = jnp.where(first_half, kr, 0.0).astype(BF16)
    kr_odd = jnp.where(first_half, 0.0, kr).astype(BF16)
    k_all = _dot(ckv, wuk_ref[...]).astype(BF16)
    for hd in range(H_C):
        k_ref[hd, :, 0:QK_NOPE] = k_all[:, hd * QK_NOPE:(hd + 1) * QK_NOPE]
        k_ref[hd, :, QK_NOPE:QK_PAD] = kr_even if hd % 2 == 0 else kr_odd

    vt = _dot_nt(wuvt_ref[...], ckv)
    ones_rows = (lax.broadcasted_iota(jnp.int32, (V_AUG - V_HEAD, tm), 0) == 0).astype(BF16)
    for hd in range(H_C):
        vt_ref[hd, 0:V_HEAD, :] = vt[hd * V_HEAD:(hd + 1) * V_HEAD, :].astype(BF16)
        vt_ref[hd, V_HEAD:V_AUG, :] = ones_rows

    scale = (QK_NOPE + QK_ROPE) ** -0.5 * math.log2(math.e)
    cq = _rms(_dot(h, wq_ref[...]), gq_ref[...]).astype(BF16)
    qn = _dot_nt(wuqt_ref[0:H_C * QK_NOPE, :], cq) * scale
    qr = _dot_nt(wuqt_ref[H_C * QK_NOPE:H_C * (QK_NOPE + QK_ROPE), :], cq) * scale
    zeros = jnp.zeros((QK_ROPE, tm), BF16)
    for hd in range(H_C):
        qt_ref[hd, 0:QK_NOPE, :] = qn[hd * QK_NOPE:(hd + 1) * QK_NOPE, :].astype(BF16)
        x1 = qr[hd * QK_ROPE:hd * QK_ROPE + half, :]
        x2 = qr[hd * QK_ROPE + half:(hd + 1) * QK_ROPE, :]
        lo = QK_NOPE if hd % 2 == 0 else QK_NOPE + QK_ROPE
        pad = QK_NOPE + QK_ROPE if hd % 2 == 0 else QK_NOPE
        qt_ref[hd, lo:lo + half, :] = (x1 * ct - x2 * st).astype(BF16)
        qt_ref[hd, lo + half:lo + QK_ROPE, :] = (x1 * st + x2 * ct).astype(BF16)
        qt_ref[hd, pad:pad + QK_ROPE, :] = zeros


def _odd_in(x2, g_pre, w_segs, g_q, w_uq_t, g_kv, w_uk, w_uv_t, cos_t, sin_t, B, S, tm):
    T = B * S
    nt = S // tm
    full = lambda a: pl.BlockSpec(a.shape, lambda i: (0,) * a.ndim)
    row = lambda w: pl.BlockSpec((tm, w), lambda i: (i, 0))
    return pl.pallas_call(
        _odd_in_kernel,
        grid=(T // tm,),
        in_specs=[row(D_MODEL), full(g_pre)] + [full(w) for w in w_segs]
                 + [full(g_q), full(w_uq_t), full(g_kv), full(w_uk), full(w_uv_t),
                    pl.BlockSpec((QK_ROPE // 2, tm), lambda i: (0, i)),
                    pl.BlockSpec((QK_ROPE // 2, tm), lambda i: (0, i))],
        out_specs=[pl.BlockSpec((None, H_C, QK_PAD, tm), lambda i: (i // nt, 0, 0, i % nt)),
                   pl.BlockSpec((None, H_C, tm, QK_PAD), lambda i: (i // nt, 0, i % nt, 0)),
                   pl.BlockSpec((None, H_C, None, V_AUG, tm), lambda i: (i // nt, 0, i % nt, 0, 0)),
                   pl.BlockSpec((D_GROUPS, None, tm, D_GROUP_CH), lambda i: (0, i // nt, i % nt, 0)),
                   row(W_C), row(W_D)],
        out_shape=[
            jax.ShapeDtypeStruct((B, H_C, QK_PAD, S), BF16),
            jax.ShapeDtypeStruct((B, H_C, S, QK_PAD), BF16),
            jax.ShapeDtypeStruct((B, H_C, nt, V_AUG, tm), BF16),
            jax.ShapeDtypeStruct((D_GROUPS, B, S, D_GROUP_CH), BF16),
            jax.ShapeDtypeStruct((T, W_C), BF16),
            jax.ShapeDtypeStruct((T, W_D), BF16),
        ],
        compiler_params=_params(("arbitrary",)),
        name="odd_in_proj",
    )(x2, g_pre, *w_segs, g_q, w_uq_t, g_kv, w_uk, w_uv_t, cos_t, sin_t)


def _attn_kernel(qt_ref, k_ref, vt_ref, o_ref, s_a, s_b, p_a, p_b, acc_scr, *, tk):
    ts = s_a.shape[0]
    per = ts // tk
    nk = k_ref.shape[0] // tk
    qt = qt_ref[...]

    s_bufs = (s_a, s_b)
    p_bufs = (p_a, p_b)

    def scores(blk):
        return _dot(k_ref[blk * ts:(blk + 1) * ts, :], qt)

    tv = vt_ref.shape[-1]

    def v_tile(j):
        return vt_ref[j * tk // tv][:, j * tk % tv:j * tk % tv + tk]

    s_bufs[0][...] = scores(0)
    m = None
    alpha_prev = None
    for j in range(nk):
        blk, sub = divmod(j, per)
        if sub == 0 and (blk + 1) * per < nk:
            s_bufs[(blk + 1) % 2][...] = scores(blk + 1)
        s = s_bufs[blk % 2][sub * tk:(sub + 1) * tk, :]
        tile_max = jnp.max(s, axis=0, keepdims=True)
        m_new = tile_max if m is None else jnp.maximum(m, tile_max)
        if j >= 1:
            pv = _dot(v_tile(j - 1), p_bufs[(j - 1) % 2][...])
            acc_scr[...] = pv if j == 1 else alpha_prev * acc_scr[...] + pv
        alpha_prev = None if m is None else jnp.exp2(m - m_new)
        p_bufs[j % 2][...] = jnp.exp2((s - m_new).astype(BF16))
        m = m_new
    acc = alpha_prev * acc_scr[...] + _dot(v_tile(nk - 1), p_bufs[(nk - 1) % 2][...])
    o_ref[...] = (acc[0:V_HEAD, :] / acc[V_HEAD:V_HEAD + 1, :]).T.astype(BF16)


def _attention(qt, k, vt, B, S, tq, tk, ts):
    nv, tv = vt.shape[2], vt.shape[4]
    assert S // tk >= 2 and tv % tk == 0 and ts % tk == 0 and S % ts == 0
    return pl.pallas_call(
        functools.partial(_attn_kernel, tk=tk),
        grid=(B, H_C, S // tq),
        in_specs=[
            pl.BlockSpec((None, None, QK_PAD, tq), lambda b, h, i: (b, h, 0, i)),
            pl.BlockSpec((None, None, S, QK_PAD), lambda b, h, i: (b, h, 0, 0)),
            pl.BlockSpec((None, None, nv, V_AUG, tv), lambda b, h, i: (b, h, 0, 0, 0)),
        ],
        out_specs=pl.BlockSpec((None, tq, V_HEAD), lambda b, h, i: (b, i, h)),
        out_shape=jax.ShapeDtypeStruct((B, S, W_C), BF16),
        scratch_shapes=[pltpu.VMEM((ts, tq), F32), pltpu.VMEM((ts, tq), F32),
                        pltpu.VMEM((tk, tq), BF16), pltpu.VMEM((tk, tq), BF16),
                        pltpu.VMEM((V_AUG, tq), F32)],
        compiler_params=_params(("arbitrary", "arbitrary", "arbitrary")),
        name="flash_attention",
    )(qt, k, vt)


def _fft_tables(S):
    n1 = S // LANES
    a = np.arange(n1, dtype=np.float64)
    ang1 = 2.0 * np.pi * np.outer(a, a) / n1
    w1 = np.concatenate([np.cos(ang1), -np.sin(ang1)], axis=0)
    k1 = np.arange(n1)[:, None, None]
    k2 = np.arange(LANES)[None, :, None]
    n2 = np.arange(LANES)[None, None, :]
    kk = (k1 + n1 * k2) * n2 % S
    ang2 = 2.0 * np.pi * kk.astype(np.float64) / S
    gc = np.cos(ang2).reshape(n1 * LANES, LANES)
    gs = np.sin(ang2).reshape(n1 * LANES, LANES)
    c = np.arange(D_GROUP_CH, dtype=np.float64)
    angc = 2.0 * np.pi * np.outer(c, c) / D_GROUP_CH
    f32 = lambda t: jnp.asarray(t.astype(np.float32))
    return f32(w1), f32(gc), f32(gs), f32(np.cos(angc)), f32(np.sin(angc))


def _fft_pitch(n1):
    return n1 + 8


def _fft_kernel(x_ref, w1_ref, gc_ref, gs_ref, cc_ref, sc_ref, out_ref, ar_scr, ai_scr, z_scr, *, n1, cw):
    n_chunks = (LANES * LANES) // cw
    per = cw // LANES
    pitch = _fft_pitch(n1)
    w1 = w1_ref[...]

    def stage1(ch, carry):
        lo = pl.multiple_of(ch * cw, cw)
        res = _dot(w1, x_ref[:, pl.ds(lo, cw)])
        for j in range(per):
            r0 = pl.multiple_of((ch * per + j) * pitch, 8) if pitch % 8 == 0 else (ch * per + j) * pitch
            ar_scr[pl.ds(r0, n1), :] = res[0:n1, j * LANES:(j + 1) * LANES]
            ai_scr[pl.ds(r0, n1), :] = res[n1:2 * n1, j * LANES:(j + 1) * LANES]
        return carry

    lax.fori_loop(0, n_chunks, stage1, 0)

    ccm = cc_ref[...]
    scm = sc_ref[...]
    norm = 1.0 / math.sqrt(n1 * LANES * D_GROUP_CH)

    def sequence_dft(k1):
        ar = ar_scr[pl.ds(k1, LANES, stride=pitch), :]
        ai = ai_scr[pl.ds(k1, LANES, stride=pitch), :]
        a = jnp.concatenate([ar, ai], axis=1).astype(BF16)
        t0 = pl.multiple_of(k1 * LANES, LANES)
        p1 = _dot(gc_ref[pl.ds(t0, LANES), :], a)
        p2 = _dot(gs_ref[pl.ds(t0, LANES), :], a)
        zr = p1[:, 0:LANES] + p2[:, LANES:2 * LANES]
        zi = p1[:, LANES:2 * LANES] - p2[:, 0:LANES]
        return zr.astype(BF16), zi.astype(BF16)

    def channel_dft(k1, z):
        fr = _dot(z[0], ccm) + _dot(z[1], scm)
        z_scr[pl.ds(k1, LANES, stride=pitch), :] = fr * norm

    group = 8 if n1 % 8 == 0 else 1

    def stage2(i, carry):
        zs = [sequence_dft(i * group + u) for u in range(group)]
        for u in range(group):
            channel_dft(i * group + u, zs[u])
        return carry

    lax.fori_loop(0, n1 // group, stage2, 0)

    for k2 in range(LANES):
        out_ref[k2 * n1:(k2 + 1) * n1, :] = z_scr[k2 * pitch:k2 * pitch + n1, :]


def _fft(f4, B, S):
    n1 = S // LANES
    w1, gc, gs, cc, sc = _fft_tables(S)
    bf = lambda t: t.astype(BF16)
    x4 = f4.reshape(D_GROUPS, B, n1, LANES * D_GROUP_CH)
    cw = 2048
    full = lambda a: pl.BlockSpec(a.shape, lambda g, b: (0,) * a.ndim)
    tabs = [bf(w1), bf(gc), bf(gs), bf(cc), bf(sc)]
    return pl.pallas_call(
        functools.partial(_fft_kernel, n1=n1, cw=cw),
        grid=(D_GROUPS, B),
        in_specs=[pl.BlockSpec((None, None, n1, LANES * D_GROUP_CH), lambda g, b: (g, b, 0, 0))]
                 + [full(t) for t in tabs],
        out_specs=pl.BlockSpec((None, S, D_GROUP_CH), lambda g, b: (b, 0, g)),
        out_shape=jax.ShapeDtypeStruct((B, S, W_D), F32),
        scratch_shapes=[pltpu.VMEM((LANES * _fft_pitch(n1), D_GROUP_CH), F32)] * 3,
        compiler_params=_params(("arbitrary", "arbitrary")),
        name="fft2_real",
    )(x4, *tabs)


def _odd_out_kernel(att_ref, zc_ref, fr_ref, zd_ref, x_ref, wfd_ref, w_ref, g_ref, out_ref):
    yc = (att_ref[...].astype(F32) * _silu(zc_ref[...].astype(F32))).astype(BF16)
    yd = (_dot(fr_ref[...].astype(BF16), wfd_ref[...]) * _silu(zd_ref[...].astype(F32))).astype(BF16)
    y = _dot(yc, w_ref[0:W_C, :]) + _dot(yd, w_ref[W_C:W_C + W_D, :])
    out_ref[...] = x_ref[...] + _rms(y, g_ref[...])


def _odd_out(att, zc, fr, zd, x2, w_fd, w_out, g_post, tm):
    T = x2.shape[0]
    row = lambda w: pl.BlockSpec((tm, w), lambda i: (i, 0))
    full = lambda a: pl.BlockSpec(a.shape, lambda i: (0,) * a.ndim)
    return pl.pallas_call(
        _odd_out_kernel,
        grid=(T // tm,),
        in_specs=[row(W_C), row(W_C), row(W_D), row(W_D), row(D_MODEL), full(w_fd), full(w_out), full(g_post)],
        out_specs=row(D_MODEL),
        out_shape=jax.ShapeDtypeStruct((T, D_MODEL), F32),
        compiler_params=_params(("arbitrary",)),
        name="odd_out_proj",
    )(att, zc, fr, zd, x2, w_fd, w_out, g_post)


def _even_layer(x2, B, S, g_pre, g_post, w_in, b_gate, conv_w, conv_b, gn_g, gn_b, hn_g, w_out):
    o = np.cumsum([0, 3 * W_A, H_B * DQK_B, H_B * DQK_B, W_B, 2 * W_B, 4 * H_B])
    seg = lambda i: w_in[:, o[i]:o[i + 1]].astype(BF16)
    w_gate = jnp.pad(seg(5), ((0, 0), (0, LANES - 4 * H_B)))
    bg = jnp.pad(b_gate, (0, LANES - 4 * H_B)).reshape(1, LANES)
    tm = min(512, S)
    pa, poz, pk, qt, vt, gates, gates_t = _even_in(
        x2, g_pre.reshape(1, -1), seg(0), seg(4), seg(2), seg(1).T, seg(3).T, w_gate, bg,
        seg(5).T, b_gate.reshape(-1, 1), tm)
    ya = _conv_module(pa, conv_w, conv_b.reshape(1, -1), gn_g.reshape(1, -1), gn_b.reshape(1, -1),
                      B, S, min(512, S))
    hf, hb = _mlstm(pk, qt, vt, gates, gates_t, B, S, min(MLSTM_CHUNK, S))
    return _even_out(ya, hf, hb, poz, x2, hn_g.reshape(1, -1), w_out.astype(BF16), g_post.reshape(1, -1),
                     min(512, S))


def _odd_layer(x2, B, S, cos, sin, g_pre, g_post, w_in, g_q, w_uq, g_kv, w_ukv, w_fd, w_out):
    T = B * S
    o = np.cumsum([0, Q_LORA, KV_LORA, QK_ROPE, W_D, W_C, W_D])
    seg = lambda i: w_in[:, o[i]:o[i + 1]].astype(BF16)
    w_segs = [seg(0), seg(1), seg(3), seg(4), seg(5), seg(2).T]
    wq = w_uq.reshape(Q_LORA, H_C, QK_NOPE + QK_ROPE)
    w_uq_t = jnp.concatenate([wq[:, :, :QK_NOPE].reshape(Q_LORA, -1),
                              wq[:, :, QK_NOPE:].reshape(Q_LORA, -1)], axis=1).T.astype(BF16)
    wkv = w_ukv.reshape(KV_LORA, H_C, QK_NOPE + V_HEAD)
    w_uk = wkv[:, :, :QK_NOPE].reshape(KV_LORA, -1).astype(BF16)
    w_uv_t = wkv[:, :, QK_NOPE:].reshape(KV_LORA, -1).T.astype(BF16)
    tm = min(512, S // 2)
    qt, k, vt, f4, zc, zd = _odd_in(x2, g_pre.reshape(1, -1), w_segs, g_q.reshape(1, -1), w_uq_t,
                                    g_kv.reshape(1, -1), w_uk, w_uv_t, cos, sin, B, S, tm)
    att = _attention(qt, k, vt, B, S, min(ATTN_TQ, S), min(ATTN_TK, tm), min(ATTN_TS, S)).reshape(T, W_C)
    fr = _fft(f4, B, S).reshape(T, W_D)
    return _odd_out(att, zc, fr, zd, x2, w_fd.astype(BF16), w_out.astype(BF16), g_post.reshape(1, -1), tm)


def _rope_tables(positions):
    inv = ROPE_THETA ** (-jnp.arange(0, QK_ROPE, 2, dtype=F32) / QK_ROPE)
    ang = inv[:, None] * positions.astype(F32).reshape(1, -1)
    return jnp.cos(ang), jnp.sin(ang)


def kernel(x, positions, even_g_pre, even_g_post, even_w_in, even_b_gate, even_conv_w, even_conv_b,
           even_gn_g, even_gn_b, even_hn_g, even_w_out, odd_g_pre, odd_g_post, odd_w_in, odd_g_q,
           odd_w_uq, odd_g_kv, odd_w_ukv, odd_w_fd, odd_w_out):
    B, S, _ = x.shape
    cos, sin = _rope_tables(positions)
    h = x.reshape(B * S, D_MODEL)
    depth = even_w_in.shape[0] + odd_w_in.shape[0]
    for layer in range(depth):
        j = layer // 2
        if layer % 2 == 0:
            h = _even_layer(h, B, S, even_g_pre[j], even_g_post[j], even_w_in[j], even_b_gate[j],
                            even_conv_w[j], even_conv_b[j], even_gn_g[j], even_gn_b[j], even_hn_g[j],
                            even_w_out[j])
        else:
            h = _odd_layer(h, B, S, cos, sin, odd_g_pre[j], odd_g_post[j], odd_w_in[j], odd_g_q[j],
                           odd_w_uq[j], odd_g_kv[j], odd_w_ukv[j], odd_w_fd[j], odd_w_out[j])
    return h.reshape(B, S, D_MODEL)
```

```python
import functools
import math

import numpy as np
import jax
import jax.numpy as jnp
from jax import lax
from jax.experimental import pallas as pl
from jax.experimental.pallas import tpu as pltpu

F32 = jnp.float32
BF16 = jnp.bfloat16

D_MODEL = 1024
RMS_EPS = 1e-6
LN_EPS = 1e-5
W_A = 1024
CONV_K = 31
A_GROUPS = 8
H_B = 4
DQK_B = 128
DV_B = 256
W_B = 1024
H_C = 8
QK_NOPE = 128
QK_ROPE = 64
V_HEAD = 128
Q_LORA = 384
KV_LORA = 256
W_C = 1024
ROPE_THETA = 10000.0
D_GROUPS = 4
D_GROUP_CH = 128
W_D = 512

LANES = 128
HALO = 16
QK_PAD = 256
V_AUG = V_HEAD + 16
VMEM_LIMIT = 56 * 1024 * 1024

MLSTM_CHUNK = 256
MLSTM_AUG = DV_B + 16
ATTN_TQ = 2048
ATTN_TK = 256
NEG_BIG = -1e30


def _params(sem, vmem=VMEM_LIMIT):
    return pltpu.CompilerParams(dimension_semantics=sem, vmem_limit_bytes=vmem)


def _sigmoid(x):
    return 1.0 / (1.0 + jnp.exp(-x))


def _silu(x):
    return x * _sigmoid(x)


def _log_sigmoid(x):
    return jnp.minimum(x, 0.0) - jnp.log(1.0 + jnp.exp(-jnp.abs(x)))


def _rms(x, g):
    return x * lax.rsqrt(jnp.mean(x * x, axis=-1, keepdims=True) + RMS_EPS) * g


def _dot(a, b):
    return jnp.dot(a, b, preferred_element_type=F32)


def _dot_nt(a, b):
    return lax.dot_general(a, b, (((1,), (1,)), ((), ())), preferred_element_type=F32)


def _split3(x):
    hi = x.astype(BF16)
    r1 = x - hi.astype(F32)
    mid = r1.astype(BF16)
    lo = (r1 - mid.astype(F32)).astype(BF16)
    return hi, mid, lo


def _tri_sum_left(tri, x):
    hi, mid, lo = _split3(x)
    return _dot(tri, lo) + _dot(tri, mid) + _dot(tri, hi)


def _tri_sum_right(x, tri):
    hi, mid, lo = _split3(x)
    return _dot(lo, tri) + _dot(mid, tri) + _dot(hi, tri)


def _even_in_kernel(x_ref, g_ref, wa_ref, woz_ref, wk_ref, wqt_ref, wvt_ref, wg_ref, bg_ref, wgt_ref, bgt_ref,
                    pa_ref, poz_ref, pk_ref, qt_ref, vt_ref, gates_ref, gates_t_ref):
    h = _rms(x_ref[...], g_ref[...]).astype(BF16)
    chunk = 1024
    for w_ref, p_ref in ((wa_ref, pa_ref), (woz_ref, poz_ref), (wk_ref, pk_ref)):
        for c0 in range(0, w_ref.shape[1], chunk):
            c1 = min(c0 + chunk, w_ref.shape[1])
            p_ref[:, c0:c1] = _dot(h, w_ref[:, c0:c1]).astype(BF16)
    qt_ref[...] = _dot_nt(wqt_ref[...], h).astype(BF16)
    vt_ref[...] = _dot_nt(wvt_ref[...], h).astype(BF16)
    gates_ref[...] = _dot(h, wg_ref[...]) + bg_ref[...]
    gates_t_ref[...] = _dot_nt(wgt_ref[...], h) + bgt_ref[...]


def _even_in(x2, g_pre, w_a, w_oz, w_k, w_qt, w_vt, w_gate, b_gate, w_gate_t, b_gate_t, tm):
    T = x2.shape[0]
    full = lambda a: pl.BlockSpec(a.shape, lambda i: (0, 0))
    row = lambda w: pl.BlockSpec((tm, w), lambda i: (i, 0))
    col = lambda r: pl.BlockSpec((r, tm), lambda i: (0, i))
    weights = (w_a, w_oz, w_k, w_qt, w_vt, w_gate, b_gate, w_gate_t, b_gate_t)
    return pl.pallas_call(
        _even_in_kernel,
        grid=(T // tm,),
        in_specs=[row(D_MODEL), full(g_pre)] + [full(w) for w in weights],
        out_specs=[row(w_a.shape[1]), row(w_oz.shape[1]), row(w_k.shape[1]),
                   col(H_B * DQK_B), col(W_B), row(LANES), col(4 * H_B)],
        out_shape=[
            jax.ShapeDtypeStruct((T, w_a.shape[1]), BF16),
            jax.ShapeDtypeStruct((T, w_oz.shape[1]), BF16),
            jax.ShapeDtypeStruct((T, w_k.shape[1]), BF16),
            jax.ShapeDtypeStruct((H_B * DQK_B, T), BF16),
            jax.ShapeDtypeStruct((W_B, T), BF16),
            jax.ShapeDtypeStruct((T, LANES), F32),
            jax.ShapeDtypeStruct((4 * H_B, T), F32),
        ],
        compiler_params=_params(("arbitrary",)),
        name="even_in_proj",
    )(x2, g_pre, *weights)


CONV_FIRST_TAP = HALO - CONV_K // 2
CONV_SUB = 8
CONV_TILE_TAPS = (CONV_FIRST_TAP + CONV_K - 1) // CONV_SUB + 1


def _conv_shift_matrix(rc):
    span = rc + (CONV_TILE_TAPS - 1) * CONV_SUB
    win = rc + 2 * HALO
    m = np.zeros((CONV_SUB * span, win), np.float32)
    for b in range(CONV_SUB):
        m[b * span + np.arange(span), np.arange(span) + b] = 1.0
    return jnp.asarray(m).astype(BF16)


def _conv_kernel(av_ref, ag_ref, za_ref, avp_ref, agp_ref, avn_ref, agn_ref,
                 cw_ref, cb_ref, gg_ref, gb_ref, sm_ref, out_ref, u_scr, sh_scr, *, ts, rc):
    i = pl.program_id(1)
    last = pl.num_programs(1) - 1

    def gated(a_ref, g_ref):
        return a_ref[...].astype(F32) * _sigmoid(g_ref[...].astype(F32))

    u_scr[HALO:HALO + ts, :] = gated(av_ref, ag_ref).astype(BF16)
    u_scr[0:HALO, :] = jnp.where(i > 0, gated(avp_ref, agp_ref), 0.0).astype(BF16)
    u_scr[HALO + ts:HALO + ts + HALO, :] = jnp.where(i < last, gated(avn_ref, agn_ref), 0.0).astype(BF16)

    span = rc + (CONV_TILE_TAPS - 1) * CONV_SUB
    for r0 in range(0, ts, rc):
        sh_scr[...] = _dot(sm_ref[...], u_scr[r0:r0 + rc + 2 * HALO, :])
        for g in range(A_GROUPS):
            cs = slice(g * LANES, (g + 1) * LANES)
            acc = jnp.zeros((rc, LANES), F32) + cb_ref[:, cs]
            for b in range(CONV_SUB):
                for a in range(CONV_TILE_TAPS):
                    j = CONV_SUB * a + b - CONV_FIRST_TAP
                    if 0 <= j < CONV_K:
                        lo = b * span + CONV_SUB * a
                        acc = acc + sh_scr[lo:lo + rc, cs] * cw_ref[j:j + 1, cs]
            mu = jnp.mean(acc, axis=-1, keepdims=True)
            xc = acc - mu
            var = jnp.mean(xc * xc, axis=-1, keepdims=True)
            y = xc * lax.rsqrt(var + LN_EPS) * gg_ref[:, cs] + gb_ref[:, cs]
            y = _silu(y) * _silu(za_ref[r0:r0 + rc, cs].astype(F32))
            out_ref[r0:r0 + rc, cs] = y.astype(BF16)


def _conv_module(p, conv_w, conv_b, gn_g, gn_b, B, S, ts):
    T = B * S
    nt = S // ts
    hb = ts // HALO
    n_hblk = T // HALO

    def cur(col):
        return pl.BlockSpec((ts, W_A), lambda b, i: (b * nt + i, col))

    def prev(col):
        return pl.BlockSpec((HALO, W_A), lambda b, i: (jnp.maximum((b * nt + i) * hb - 1, 0), col))

    def nxt(col):
        return pl.BlockSpec((HALO, W_A), lambda b, i: (jnp.minimum((b * nt + i + 1) * hb, n_hblk - 1), col))

    def full(r):
        return pl.BlockSpec((r, W_A), lambda b, i: (0, 0))

    rc = min(ts, 128)
    shift = _conv_shift_matrix(rc)
    return pl.pallas_call(
        functools.partial(_conv_kernel, ts=ts, rc=rc),
        grid=(B, nt),
        in_specs=[cur(0), cur(1), cur(2), prev(0), prev(1), nxt(0), nxt(1),
                  full(CONV_K), full(1), full(1), full(1),
                  pl.BlockSpec(shift.shape, lambda b, i: (0, 0))],
        out_specs=pl.BlockSpec((ts, W_A), lambda b, i: (b * nt + i, 0)),
        out_shape=jax.ShapeDtypeStruct((T, W_A), BF16),
        scratch_shapes=[pltpu.VMEM((ts + 2 * HALO, W_A), BF16),
                        pltpu.VMEM((shift.shape[0], W_A), F32)],
        compiler_params=_params(("arbitrary", "arbitrary")),
        name="conv_module",
    )(p, p, p, p, p, p, p, conv_w, conv_b, gn_g, gn_b, shift)


def _mlstm_kernel(qtf_ref, kf_ref, vtf_ref, gcf_ref, grf_ref,
                  qtb_ref, kb_ref, vtb_ref, gcb_ref, grb_ref,
                  hf_ref, hb_ref, st_scr, m_scr, *, L):
    c = pl.program_id(1)

    @pl.when(c == 0)
    def _():
        st_scr[...] = jnp.zeros_like(st_scr)
        m_scr[...] = jnp.zeros_like(m_scr)

    row = lax.broadcasted_iota(jnp.int32, (L, L), 0)
    col = lax.broadcasted_iota(jnp.int32, (L, L), 1)
    lower = col <= row
    upper = col >= row
    lower_f = lower.astype(BF16)
    upper_f = upper.astype(BF16)
    scale = DQK_B ** -0.5
    ones_rows = (lax.broadcasted_iota(jnp.int32, (MLSTM_AUG - DV_B, L), 0) == 0).astype(BF16)

    streams = []
    for fwd, (qt_ref, k_ref, vt_ref, gc_ref, gr_ref, h_ref) in (
            (True, (qtf_ref, kf_ref, vtf_ref, gcf_ref, grf_ref, hf_ref)),
            (False, (qtb_ref, kb_ref, vtb_ref, gcb_ref, grb_ref, hb_ref))):
        gate_i = 0 if fwd else 2 * H_B
        gate_f = gate_i + H_B
        gc = gc_ref[...]
        gr = gr_ref[...]
        b_col_all = _tri_sum_left(lower_f if fwd else upper_f, _log_sigmoid(gc))
        b_row_all = _tri_sum_right(_log_sigmoid(gr), upper_f if fwd else lower_f)
        for h in range(H_B):
            b_col = b_col_all[:, gate_f + h:gate_f + h + 1]
            streams.append(dict(
                fwd=fwd, h=h, sidx=h if fwd else H_B + h, h_ref=h_ref, k_ref=k_ref, vt_ref=vt_ref,
                mask_t=upper if fwd else lower,
                b_col=b_col,
                b_row=b_row_all[gate_f + h:gate_f + h + 1, :],
                i_col=gc[:, gate_i + h:gate_i + h + 1],
                i_row=gr[gate_i + h:gate_i + h + 1, :],
                b_end=b_col[L - 1:L, :] if fwd else b_col[0:1, :],
                qt=(qt_ref[h * DQK_B:(h + 1) * DQK_B, :].astype(F32) * scale).astype(BF16)))

    def keys(s):
        return s["k_ref"][:, s["h"] * DQK_B:(s["h"] + 1) * DQK_B]

    def values(s):
        v = s["vt_ref"][s["h"] * DV_B:(s["h"] + 1) * DV_B, :]
        return jnp.concatenate([v, ones_rows], axis=0)

    for s in streams:
        s["qk"] = _dot(keys(s), s["qt"])
        s["inter"] = _dot(st_scr[s["sidx"]].astype(BF16), s["qt"])
    for s in streams:
        m = m_scr[s["sidx"]][:, 0:1]
        d_t = jnp.where(s["mask_t"], s["b_row"] + (s["i_col"] - s["b_col"]), -jnp.inf)
        a = s["b_row"] + m
        m_t = jnp.maximum(a, jnp.max(d_t, axis=0, keepdims=True))
        s_t = s["qk"] * jnp.exp(d_t - m_t)
        s["num"] = _dot(values(s), s_t.astype(BF16)) + jnp.exp(a - m_t) * s["inter"]
        s["m"], s["m_t"] = m, m_t
    for s in streams:
        den = s["num"][DV_B:DV_B + 1, :]
        hval = s["num"][0:DV_B, :] / jnp.maximum(jnp.abs(den), jnp.exp(-s["m_t"]))
        s["h_ref"][s["h"] * DV_B:(s["h"] + 1) * DV_B, :] = hval.astype(BF16)
    for s in streams:
        g_row = s["b_end"] - s["b_row"] + s["i_row"]
        m_new = jnp.maximum(s["b_end"] + s["m"], jnp.max(g_row, axis=1, keepdims=True))
        wk = jnp.exp(g_row - m_new)
        decay = jnp.exp(s["b_end"] + s["m"] - m_new)
        upd = _dot((values(s).astype(F32) * wk).astype(BF16), keys(s))
        st_scr[s["sidx"]] = decay * st_scr[s["sidx"]] + upd
        m_scr[s["sidx"]] = jnp.broadcast_to(m_new, (1, LANES))


def _mlstm(p, qt, vt, gates, gates_t, B, S, L):
    T = B * S
    nc = S // L

    def specs(chunk):
        return [
            pl.BlockSpec((H_B * DQK_B, L), lambda b, c: (0, b * nc + chunk(c))),
            pl.BlockSpec((L, H_B * DQK_B), lambda b, c: (b * nc + chunk(c), 0)),
            pl.BlockSpec((W_B, L), lambda b, c: (0, b * nc + chunk(c))),
            pl.BlockSpec((L, LANES), lambda b, c: (b * nc + chunk(c), 0)),
            pl.BlockSpec((4 * H_B, L), lambda b, c: (0, b * nc + chunk(c))),
        ]

    fwd = lambda c: c
    bwd = lambda c: nc - 1 - c
    n_streams = 2 * H_B
    return pl.pallas_call(
        functools.partial(_mlstm_kernel, L=L),
        grid=(B, nc),
        in_specs=specs(fwd) + specs(bwd),
        out_specs=[
            pl.BlockSpec((W_B, L), lambda b, c: (0, b * nc + c)),
            pl.BlockSpec((W_B, L), lambda b, c: (0, b * nc + nc - 1 - c)),
        ],
        out_shape=[jax.ShapeDtypeStruct((W_B, T), BF16)] * 2,
        scratch_shapes=[
            pltpu.VMEM((n_streams, MLSTM_AUG, DQK_B), F32),
            pltpu.VMEM((n_streams, 1, LANES), F32),
        ],
        compiler_params=_params(("arbitrary", "arbitrary")),
        name="mlstm",
    )(qt, p, vt, gates, gates_t, qt, p, vt, gates, gates_t)


def _even_out_kernel(ya_ref, hf_ref, hb_ref, o_ref, zb_ref, x_ref, hn_ref, w_ref, g_ref, out_ref):
    parts = []
    for h in range(H_B):
        cs = slice(h * DV_B, (h + 1) * DV_B)
        hm_t = hf_ref[cs, :].astype(F32) + hb_ref[cs, :].astype(F32)
        hm_t = hm_t * lax.rsqrt(jnp.mean(hm_t * hm_t, axis=0, keepdims=True) + RMS_EPS)
        hm = hm_t.T * hn_ref[:, cs]
        yb = _sigmoid(o_ref[:, cs].astype(F32)) * hm * _silu(zb_ref[:, cs].astype(F32))
        parts.append(yb.astype(BF16))
    y = _dot(ya_ref[...], w_ref[0:W_A, :])
    for h in range(H_B):
        y = y + _dot(parts[h], w_ref[W_A + h * DV_B:W_A + (h + 1) * DV_B, :])
    out_ref[...] = x_ref[...] + _rms(y, g_ref[...])


def _even_out(ya, hf, hb, p, x2, hn_g, w_out, g_post, tm):
    T = x2.shape[0]
    row = lambda w: pl.BlockSpec((tm, w), lambda i: (i, 0))
    feat = pl.BlockSpec((W_B, tm), lambda i: (0, i))
    return pl.pallas_call(
        _even_out_kernel,
        grid=(T // tm,),
        in_specs=[row(W_A), feat, feat,
                  pl.BlockSpec((tm, W_B), lambda i: (i, 0)),
                  pl.BlockSpec((tm, W_B), lambda i: (i, 1)),
                  row(D_MODEL),
                  pl.BlockSpec((1, W_B), lambda i: (0, 0)),
                  pl.BlockSpec((W_A + W_B, D_MODEL), lambda i: (0, 0)),
                  pl.BlockSpec((1, D_MODEL), lambda i: (0, 0))],
        out_specs=row(D_MODEL),
        out_shape=jax.ShapeDtypeStruct((T, D_MODEL), F32),
        compiler_params=_params(("arbitrary",)),
        name="even_out_proj",
    )(ya, hf, hb, p, p, x2, hn_g, w_out, g_post)


def _odd_in_kernel(x_ref, g_ref, wq_ref, wkv_ref, wf_ref, wzc_ref, wzd_ref, wkrt_ref,
                   gq_ref, wuqt_ref, gkv_ref, wuk_ref, wuvt_ref, cost_ref, sint_ref,
                   qt_ref, k_ref, vt_ref, f_ref, zc_ref, zd_ref):
    tm = x_ref.shape[0]
    h = _rms(x_ref[...], g_ref[...]).astype(BF16)
    half = QK_ROPE // 2
    ct = cost_ref[...]
    st = sint_ref[...]

    f_all = _dot(h, wf_ref[...]).astype(BF16)
    for g in range(D_GROUPS):
        f_ref[g] = f_all[:, g * D_GROUP_CH:(g + 1) * D_GROUP_CH]
    zc_ref[...] = _dot(h, wzc_ref[...]).astype(BF16)
    zd_ref[...] = _dot(h, wzd_ref[...]).astype(BF16)

    ckv = _rms(_dot(h, wkv_ref[...]), gkv_ref[...]).astype(BF16)
    lane = lax.broadcasted_iota(jnp.int32, (tm, LANES), 1)
    first_half = lane < QK_ROPE
    krt = _dot_nt(wkrt_ref[...], h)
    kr_rot = jnp.concatenate([krt[0:half] * ct - krt[half:QK_ROPE] * st,
                              krt[0:half] * st + krt[half:QK_ROPE] * ct], axis=0)
    kr = jnp.concatenate([kr_rot, kr_rot], axis=0).T
    kr_even = jnp.where(first_half, kr, 0.0).astype(BF16)
    kr_odd = jnp.where(first_half, 0.0, kr).astype(BF16)
    k_all = _dot(ckv, wuk_ref[...]).astype(BF16)
    for hd in range(H_C):
        k_ref[hd, :, 0:QK_NOPE] = k_all[:, hd * QK_NOPE:(hd + 1) * QK_NOPE]
        k_ref[hd, :, QK_NOPE:QK_PAD] = kr_even if hd % 2 == 0 else kr_odd

    vt = _dot_nt(wuvt_ref[...], ckv)
    ones_rows = (lax.broadcasted_iota(jnp.int32, (V_AUG - V_HEAD, tm), 0) == 0).astype(BF16)
    for hd in range(H_C):
        vt_ref[hd, 0:V_HEAD, :] = vt[hd * V_HEAD:(hd + 1) * V_HEAD, :].astype(BF16)
        vt_ref[hd, V_HEAD:V_AUG, :] = ones_rows

    scale = (QK_NOPE + QK_ROPE) ** -0.5 * math.log2(math.e)
    cq = _rms(_dot(h, wq_ref[...]), gq_ref[...]).astype(BF16)
    qn = _dot_nt(wuqt_ref[0:H_C * QK_NOPE, :], cq) * scale
    qr = _dot_nt(wuqt_ref[H_C * QK_NOPE:H_C * (QK_NOPE + QK_ROPE), :], cq) * scale
    zeros = jnp.zeros((QK_ROPE, tm), BF16)
    for hd in range(H_C):
        qt_ref[hd, 0:QK_NOPE, :] = qn[hd * QK_NOPE:(hd + 1) * QK_NOPE, :].astype(BF16)
        x1 = qr[hd * QK_ROPE:hd * QK_ROPE + half, :]
        x2 = qr[hd * QK_ROPE + half:(hd + 1) * QK_ROPE, :]
        lo = QK_NOPE if hd % 2 == 0 else QK_NOPE + QK_ROPE
        pad = QK_NOPE + QK_ROPE if hd % 2 == 0 else QK_NOPE
        qt_ref[hd, lo:lo + half, :] = (x1 * ct - x2 * st).astype(BF16)
        qt_ref[hd, lo + half:lo + QK_ROPE, :] = (x1 * st + x2 * ct).astype(BF16)
        qt_ref[hd, pad:pad + QK_ROPE, :] = zeros


def _odd_in(x2, g_pre, w_segs, g_q, w_uq_t, g_kv, w_uk, w_uv_t, cos_t, sin_t, B, S, tm):
    T = B * S
    nt = S // tm
    full = lambda a: pl.BlockSpec(a.shape, lambda i: (0,) * a.ndim)
    row = lambda w: pl.BlockSpec((tm, w), lambda i: (i, 0))
    return pl.pallas_call(
        _odd_in_kernel,
        grid=(T // tm,),
        in_specs=[row(D_MODEL), full(g_pre)] + [full(w) for w in w_segs]
                 + [full(g_q), full(w_uq_t), full(g_kv), full(w_uk), full(w_uv_t),
                    pl.BlockSpec((QK_ROPE // 2, tm), lambda i: (0, i)),
                    pl.BlockSpec((QK_ROPE // 2, tm), lambda i: (0, i))],
        out_specs=[pl.BlockSpec((None, H_C, QK_PAD, tm), lambda i: (i // nt, 0, 0, i % nt)),
                   pl.BlockSpec((None, H_C, tm, QK_PAD), lambda i: (i // nt, 0, i % nt, 0)),
                   pl.BlockSpec((None, H_C, None, V_AUG, tm), lambda i: (i // nt, 0, i % nt, 0, 0)),
                   pl.BlockSpec((D_GROUPS, None, tm, D_GROUP_CH), lambda i: (0, i // nt, i % nt, 0)),
                   row(W_C), row(W_D)],
        out_shape=[
            jax.ShapeDtypeStruct((B, H_C, QK_PAD, S), BF16),
            jax.ShapeDtypeStruct((B, H_C, S, QK_PAD), BF16),
            jax.ShapeDtypeStruct((B, H_C, nt, V_AUG, tm), BF16),
            jax.ShapeDtypeStruct((D_GROUPS, B, S, D_GROUP_CH), BF16),
            jax.ShapeDtypeStruct((T, W_C), BF16),
            jax.ShapeDtypeStruct((T, W_D), BF16),
        ],
        compiler_params=_params(("arbitrary",)),
        name="odd_in_proj",
    )(x2, g_pre, *w_segs, g_q, w_uq_t, g_kv, w_uk, w_uv_t, cos_t, sin_t)


def _attn_kernel(qt_ref, k_ref, vt_ref, o_ref, s_a, s_b, p_a, p_b, acc_scr, *, tk):
    nk = k_ref.shape[0] // tk
    qt = qt_ref[...]

    s_bufs = (s_a, s_b)
    p_bufs = (p_a, p_b)

    def scores(j):
        return _dot(k_ref[j * tk:(j + 1) * tk, :], qt)

    tv = vt_ref.shape[-1]

    def v_tile(j):
        return vt_ref[j * tk // tv][:, j * tk % tv:j * tk % tv + tk]

    s_bufs[0][...] = scores(0)
    m = None
    alpha_prev = None
    for j in range(nk):
        if j + 1 < nk:
            s_bufs[(j + 1) % 2][...] = scores(j + 1)
        s = s_bufs[j % 2][...]
        tile_max = jnp.max(s, axis=0, keepdims=True)
        m_new = tile_max if m is None else jnp.maximum(m, tile_max)
        if j >= 1:
            pv = _dot(v_tile(j - 1), p_bufs[(j - 1) % 2][...])
            acc_scr[...] = pv if j == 1 else alpha_prev * acc_scr[...] + pv
        alpha_prev = None if m is None else jnp.exp2(m - m_new)
        p_bufs[j % 2][...] = jnp.exp2((s - m_new).astype(BF16))
        m = m_new
    acc = alpha_prev * acc_scr[...] + _dot(v_tile(nk - 1), p_bufs[(nk - 1) % 2][...])
    o_ref[...] = (acc[0:V_HEAD, :] / acc[V_HEAD:V_HEAD + 1, :]).T.astype(BF16)


def _attention(qt, k, vt, B, S, tq, tk):
    nv, tv = vt.shape[2], vt.shape[4]
    assert S // tk >= 2 and tv % tk == 0
    return pl.pallas_call(
        functools.partial(_attn_kernel, tk=tk),
        grid=(B, H_C, S // tq),
        in_specs=[
            pl.BlockSpec((None, None, QK_PAD, tq), lambda b, h, i: (b, h, 0, i)),
            pl.BlockSpec((None, None, S, QK_PAD), lambda b, h, i: (b, h, 0, 0)),
            pl.BlockSpec((None, None, nv, V_AUG, tv), lambda b, h, i: (b, h, 0, 0, 0)),
        ],
        out_specs=pl.BlockSpec((None, tq, V_HEAD), lambda b, h, i: (b, i, h)),
        out_shape=jax.ShapeDtypeStruct((B, S, W_C), BF16),
        scratch_shapes=[pltpu.VMEM((tk, tq), F32), pltpu.VMEM((tk, tq), F32),
                        pltpu.VMEM((tk, tq), BF16), pltpu.VMEM((tk, tq), BF16),
                        pltpu.VMEM((V_AUG, tq), F32)],
        compiler_params=_params(("arbitrary", "arbitrary", "arbitrary")),
        name="flash_attention",
    )(qt, k, vt)


def _fft_tables(S):
    n1 = S // LANES
    a = np.arange(n1, dtype=np.float64)
    ang1 = 2.0 * np.pi * np.outer(a, a) / n1
    w1 = np.concatenate([np.cos(ang1), -np.sin(ang1)], axis=0)
    k1 = np.arange(n1)[:, None, None]
    k2 = np.arange(LANES)[None, :, None]
    n2 = np.arange(LANES)[None, None, :]
    kk = (k1 + n1 * k2) * n2 % S
    ang2 = 2.0 * np.pi * kk.astype(np.float64) / S
    gc = np.cos(ang2).reshape(n1 * LANES, LANES)
    gs = np.sin(ang2).reshape(n1 * LANES, LANES)
    c = np.arange(D_GROUP_CH, dtype=np.float64)
    angc = 2.0 * np.pi * np.outer(c, c) / D_GROUP_CH
    f32 = lambda t: jnp.asarray(t.astype(np.float32))
    return f32(w1), f32(gc), f32(gs), f32(np.cos(angc)), f32(np.sin(angc))


def _fft_pitch(n1):
    return n1 + 8


def _fft_kernel(x_ref, w1_ref, gc_ref, gs_ref, cc_ref, sc_ref, out_ref, ar_scr, ai_scr, z_scr, *, n1, cw):
    n_chunks = (LANES * LANES) // cw
    per = cw // LANES
    pitch = _fft_pitch(n1)
    w1 = w1_ref[...]

    def stage1(ch, carry):
        lo = pl.multiple_of(ch * cw, cw)
        res = _dot(w1, x_ref[:, pl.ds(lo, cw)])
        for j in range(per):
            r0 = pl.multiple_of((ch * per + j) * pitch, 8) if pitch % 8 == 0 else (ch * per + j) * pitch
            ar_scr[pl.ds(r0, n1), :] = res[0:n1, j * LANES:(j + 1) * LANES]
            ai_scr[pl.ds(r0, n1), :] = res[n1:2 * n1, j * LANES:(j + 1) * LANES]
        return carry

    lax.fori_loop(0, n_chunks, stage1, 0)

    ccm = cc_ref[...]
    scm = sc_ref[...]
    norm = 1.0 / math.sqrt(n1 * LANES * D_GROUP_CH)

    def sequence_dft(k1):
        ar = ar_scr[pl.ds(k1, LANES, stride=pitch), :]
        ai = ai_scr[pl.ds(k1, LANES, stride=pitch), :]
        a = jnp.concatenate([ar, ai], axis=1).astype(BF16)
        t0 = pl.multiple_of(k1 * LANES, LANES)
        p1 = _dot(gc_ref[pl.ds(t0, LANES), :], a)
        p2 = _dot(gs_ref[pl.ds(t0, LANES), :], a)
        zr = p1[:, 0:LANES] + p2[:, LANES:2 * LANES]
        zi = p1[:, LANES:2 * LANES] - p2[:, 0:LANES]
        return zr.astype(BF16), zi.astype(BF16)

    def channel_dft(k1, z):
        fr = _dot(z[0], ccm) + _dot(z[1], scm)
        z_scr[pl.ds(k1, LANES, stride=pitch), :] = fr * norm

    group = 8 if n1 % 8 == 0 else 1

    def stage2(i, carry):
        zs = [sequence_dft(i * group + u) for u in range(group)]
        for u in range(group):
            channel_dft(i * group + u, zs[u])
        return carry

    lax.fori_loop(0, n1 // group, stage2, 0)

    for k2 in range(LANES):
        out_ref[k2 * n1:(k2 + 1) * n1, :] = z_scr[k2 * pitch:k2 * pitch + n1, :]


def _fft(f4, B, S):
    n1 = S // LANES
    w1, gc, gs, cc, sc = _fft_tables(S)
    bf = lambda t: t.astype(BF16)
    x4 = f4.reshape(D_GROUPS, B, n1, LANES * D_GROUP_CH)
    cw = 2048
    full = lambda a: pl.BlockSpec(a.shape, lambda g, b: (0,) * a.ndim)
    tabs = [bf(w1), bf(gc), bf(gs), bf(cc), bf(sc)]
    return pl.pallas_call(
        functools.partial(_fft_kernel, n1=n1, cw=cw),
        grid=(D_GROUPS, B),
        in_specs=[pl.BlockSpec((None, None, n1, LANES * D_GROUP_CH), lambda g, b: (g, b, 0, 0))]
                 + [full(t) for t in tabs],
        out_specs=pl.BlockSpec((None, S, D_GROUP_CH), lambda g, b: (b, 0, g)),
        out_shape=jax.ShapeDtypeStruct((B, S, W_D), F32),
        scratch_shapes=[pltpu.VMEM((LANES * _fft_pitch(n1), D_GROUP_CH), F32)] * 3,
        compiler_params=_params(("arbitrary", "arbitrary")),
        name="fft2_real",
    )(x4, *tabs)


def _odd_out_kernel(att_ref, zc_ref, fr_ref, zd_ref, x_ref, wfd_ref, w_ref, g_ref, out_ref):
    yc = (att_ref[...].astype(F32) * _silu(zc_ref[...].astype(F32))).astype(BF16)
    yd = (_dot(fr_ref[...].astype(BF16), wfd_ref[...]) * _silu(zd_ref[...].astype(F32))).astype(BF16)
    y = _dot(yc, w_ref[0:W_C, :]) + _dot(yd, w_ref[W_C:W_C + W_D, :])
    out_ref[...] = x_ref[...] + _rms(y, g_ref[...])


def _odd_out(att, zc, fr, zd, x2, w_fd, w_out, g_post, tm):
    T = x2.shape[0]
    row = lambda w: pl.BlockSpec((tm, w), lambda i: (i, 0))
    full = lambda a: pl.BlockSpec(a.shape, lambda i: (0,) * a.ndim)
    return pl.pallas_call(
        _odd_out_kernel,
        grid=(T // tm,),
        in_specs=[row(W_C), row(W_C), row(W_D), row(W_D), row(D_MODEL), full(w_fd), full(w_out), full(g_post)],
        out_specs=row(D_MODEL),
        out_shape=jax.ShapeDtypeStruct((T, D_MODEL), F32),
        compiler_params=_params(("arbitrary",)),
        name="odd_out_proj",
    )(att, zc, fr, zd, x2, w_fd, w_out, g_post)


def _even_layer(x2, B, S, g_pre, g_post, w_in, b_gate, conv_w, conv_b, gn_g, gn_b, hn_g, w_out):
    o = np.cumsum([0, 3 * W_A, H_B * DQK_B, H_B * DQK_B, W_B, 2 * W_B, 4 * H_B])
    seg = lambda i: w_in[:, o[i]:o[i + 1]].astype(BF16)
    w_gate = jnp.pad(seg(5), ((0, 0), (0, LANES - 4 * H_B)))
    bg = jnp.pad(b_gate, (0, LANES - 4 * H_B)).reshape(1, LANES)
    tm = min(512, S)
    pa, poz, pk, qt, vt, gates, gates_t = _even_in(
        x2, g_pre.reshape(1, -1), seg(0), seg(4), seg(2), seg(1).T, seg(3).T, w_gate, bg,
        seg(5).T, b_gate.reshape(-1, 1), tm)
    ya = _conv_module(pa, conv_w, conv_b.reshape(1, -1), gn_g.reshape(1, -1), gn_b.reshape(1, -1),
                      B, S, min(512, S))
    hf, hb = _mlstm(pk, qt, vt, gates, gates_t, B, S, min(MLSTM_CHUNK, S))
    return _even_out(ya, hf, hb, poz, x2, hn_g.reshape(1, -1), w_out.astype(BF16), g_post.reshape(1, -1),
                     min(512, S))


def _odd_layer(x2, B, S, cos, sin, g_pre, g_post, w_in, g_q, w_uq, g_kv, w_ukv, w_fd, w_out):
    T = B * S
    o = np.cumsum([0, Q_LORA, KV_LORA, QK_ROPE, W_D, W_C, W_D])
    seg = lambda i: w_in[:, o[i]:o[i + 1]].astype(BF16)
    w_segs = [seg(0), seg(1), seg(3), seg(4), seg(5), seg(2).T]
    wq = w_uq.reshape(Q_LORA, H_C, QK_NOPE + QK_ROPE)
    w_uq_t = jnp.concatenate([wq[:, :, :QK_NOPE].reshape(Q_LORA, -1),
                              wq[:, :, QK_NOPE:].reshape(Q_LORA, -1)], axis=1).T.astype(BF16)
    wkv = w_ukv.reshape(KV_LORA, H_C, QK_NOPE + V_HEAD)
    w_uk = wkv[:, :, :QK_NOPE].reshape(KV_LORA, -1).astype(BF16)
    w_uv_t = wkv[:, :, QK_NOPE:].reshape(KV_LORA, -1).T.astype(BF16)
    tm = min(512, S // 2)
    qt, k, vt, f4, zc, zd = _odd_in(x2, g_pre.reshape(1, -1), w_segs, g_q.reshape(1, -1), w_uq_t,
                                    g_kv.reshape(1, -1), w_uk, w_uv_t, cos, sin, B, S, tm)
    att = _attention(qt, k, vt, B, S, min(ATTN_TQ, S), min(ATTN_TK, tm)).reshape(T, W_C)
    fr = _fft(f4, B, S).reshape(T, W_D)
    return _odd_out(att, zc, fr, zd, x2, w_fd.astype(BF16), w_out.astype(BF16), g_post.reshape(1, -1), tm)


def _rope_tables(positions):
    inv = ROPE_THETA ** (-jnp.arange(0, QK_ROPE, 2, dtype=F32) / QK_ROPE)
    ang = inv[:, None] * positions.astype(F32).reshape(1, -1)
    return jnp.cos(ang), jnp.sin(ang)


def kernel(x, positions, even_g_pre, even_g_post, even_w_in, even_b_gate, even_conv_w, even_conv_b,
           even_gn_g, even_gn_b, even_hn_g, even_w_out, odd_g_pre, odd_g_post, odd_w_in, odd_g_q,
           odd_w_uq, odd_g_kv, odd_w_ukv, odd_w_fd, odd_w_out):
    B, S, _ = x.shape
    cos, sin = _rope_tables(positions)
    h = x.reshape(B * S, D_MODEL)
    depth = even_w_in.shape[0] + odd_w_in.shape[0]
    for layer in range(depth):
        j = layer // 2
        if layer % 2 == 0:
            h = _even_layer(h, B, S, even_g_pre[j], even_g_post[j], even_w_in[j], even_b_gate[j],
                            even_conv_w[j], even_conv_b[j], even_gn_g[j], even_gn_b[j], even_hn_g[j],
                            even_w_out[j])
        else:
            h = _odd_layer(h, B, S, cos, sin, odd_g_pre[j], odd_g_post[j], odd_w_in[j], odd_g_q[j],
                           odd_w_uq[j], odd_g_kv[j], odd_w_ukv[j], odd_w_fd[j], odd_w_out[j])
    return h.reshape(B, S, D_MODEL)
```

```python
import functools
import math

import numpy as np
import jax
import jax.numpy as jnp
from jax import lax
from jax.experimental import pallas as pl
from jax.experimental.pallas import tpu as pltpu

F32 = jnp.float32
BF16 = jnp.bfloat16

D_MODEL = 1024
RMS_EPS = 1e-6
LN_EPS = 1e-5
W_A = 1024
CONV_K = 31
A_GROUPS = 8
H_B = 4
DQK_B = 128
DV_B = 256
W_B = 1024
H_C = 8
QK_NOPE = 128
QK_ROPE = 64
V_HEAD = 128
Q_LORA = 384
KV_LORA = 256
W_C = 1024
ROPE_THETA = 10000.0
D_GROUPS = 4
D_GROUP_CH = 128
W_D = 512

LANES = 128
HALO = 16
QK_PAD = 256
V_AUG = V_HEAD + 16
VMEM_LIMIT = 56 * 1024 * 1024

MLSTM_CHUNK = 256
MLSTM_AUG = DV_B + 16
ATTN_TQ = 4096
ATTN_TK = 256
NEG_BIG = -1e30


def _params(sem, vmem=VMEM_LIMIT):
    return pltpu.CompilerParams(dimension_semantics=sem, vmem_limit_bytes=vmem)


def _sigmoid(x):
    return 1.0 / (1.0 + jnp.exp(-x))


def _silu(x):
    return x * _sigmoid(x)


def _log_sigmoid(x):
    return jnp.minimum(x, 0.0) - jnp.log(1.0 + jnp.exp(-jnp.abs(x)))


def _rms(x, g):
    return x * lax.rsqrt(jnp.mean(x * x, axis=-1, keepdims=True) + RMS_EPS) * g


def _dot(a, b):
    return jnp.dot(a, b, preferred_element_type=F32)


def _dot_nt(a, b):
    return lax.dot_general(a, b, (((1,), (1,)), ((), ())), preferred_element_type=F32)


def _split3(x):
    hi = x.astype(BF16)
    r1 = x - hi.astype(F32)
    mid = r1.astype(BF16)
    lo = (r1 - mid.astype(F32)).astype(BF16)
    return hi, mid, lo


def _tri_sum_left(tri, x):
    hi, mid, lo = _split3(x)
    return _dot(tri, lo) + _dot(tri, mid) + _dot(tri, hi)


def _tri_sum_right(x, tri):
    hi, mid, lo = _split3(x)
    return _dot(lo, tri) + _dot(mid, tri) + _dot(hi, tri)


def _even_in_kernel(x_ref, g_ref, wa_ref, woz_ref, wk_ref, wqt_ref, wvt_ref, wg_ref, bg_ref, wgt_ref, bgt_ref,
                    pa_ref, poz_ref, pk_ref, qt_ref, vt_ref, gates_ref, gates_t_ref):
    h = _rms(x_ref[...], g_ref[...]).astype(BF16)
    chunk = 1024
    for w_ref, p_ref in ((wa_ref, pa_ref), (woz_ref, poz_ref), (wk_ref, pk_ref)):
        for c0 in range(0, w_ref.shape[1], chunk):
            c1 = min(c0 + chunk, w_ref.shape[1])
            p_ref[:, c0:c1] = _dot(h, w_ref[:, c0:c1]).astype(BF16)
    qt_ref[...] = _dot_nt(wqt_ref[...], h).astype(BF16)
    vt_ref[...] = _dot_nt(wvt_ref[...], h).astype(BF16)
    gates_ref[...] = _dot(h, wg_ref[...]) + bg_ref[...]
    gates_t_ref[...] = _dot_nt(wgt_ref[...], h) + bgt_ref[...]


def _even_in(x2, g_pre, w_a, w_oz, w_k, w_qt, w_vt, w_gate, b_gate, w_gate_t, b_gate_t, tm):
    T = x2.shape[0]
    full = lambda a: pl.BlockSpec(a.shape, lambda i: (0, 0))
    row = lambda w: pl.BlockSpec((tm, w), lambda i: (i, 0))
    col = lambda r: pl.BlockSpec((r, tm), lambda i: (0, i))
    weights = (w_a, w_oz, w_k, w_qt, w_vt, w_gate, b_gate, w_gate_t, b_gate_t)
    return pl.pallas_call(
        _even_in_kernel,
        grid=(T // tm,),
        in_specs=[row(D_MODEL), full(g_pre)] + [full(w) for w in weights],
        out_specs=[row(w_a.shape[1]), row(w_oz.shape[1]), row(w_k.shape[1]),
                   col(H_B * DQK_B), col(W_B), row(LANES), col(4 * H_B)],
        out_shape=[
            jax.ShapeDtypeStruct((T, w_a.shape[1]), BF16),
            jax.ShapeDtypeStruct((T, w_oz.shape[1]), BF16),
            jax.ShapeDtypeStruct((T, w_k.shape[1]), BF16),
            jax.ShapeDtypeStruct((H_B * DQK_B, T), BF16),
            jax.ShapeDtypeStruct((W_B, T), BF16),
            jax.ShapeDtypeStruct((T, LANES), F32),
            jax.ShapeDtypeStruct((4 * H_B, T), F32),
        ],
        compiler_params=_params(("arbitrary",)),
        name="even_in_proj",
    )(x2, g_pre, *weights)


CONV_FIRST_TAP = HALO - CONV_K // 2
CONV_SUB = 8
CONV_TILE_TAPS = (CONV_FIRST_TAP + CONV_K - 1) // CONV_SUB + 1


def _conv_shift_matrix(rc):
    span = rc + (CONV_TILE_TAPS - 1) * CONV_SUB
    win = rc + 2 * HALO
    m = np.zeros((CONV_SUB * span, win), np.float32)
    for b in range(CONV_SUB):
        m[b * span + np.arange(span), np.arange(span) + b] = 1.0
    return jnp.asarray(m).astype(BF16)


def _conv_kernel(av_ref, ag_ref, za_ref, avp_ref, agp_ref, avn_ref, agn_ref,
                 cw_ref, cb_ref, gg_ref, gb_ref, sm_ref, out_ref, u_scr, sh_scr, *, ts, rc):
    i = pl.program_id(1)
    last = pl.num_programs(1) - 1

    def gated(a_ref, g_ref):
        return a_ref[...].astype(F32) * _sigmoid(g_ref[...].astype(F32))

    u_scr[HALO:HALO + ts, :] = gated(av_ref, ag_ref).astype(BF16)
    u_scr[0:HALO, :] = jnp.where(i > 0, gated(avp_ref, agp_ref), 0.0).astype(BF16)
    u_scr[HALO + ts:HALO + ts + HALO, :] = jnp.where(i < last, gated(avn_ref, agn_ref), 0.0).astype(BF16)

    span = rc + (CONV_TILE_TAPS - 1) * CONV_SUB
    for r0 in range(0, ts, rc):
        sh_scr[...] = _dot(sm_ref[...], u_scr[r0:r0 + rc + 2 * HALO, :])
        for g in range(A_GROUPS):
            cs = slice(g * LANES, (g + 1) * LANES)
            acc = jnp.zeros((rc, LANES), F32) + cb_ref[:, cs]
            for b in range(CONV_SUB):
                for a in range(CONV_TILE_TAPS):
                    j = CONV_SUB * a + b - CONV_FIRST_TAP
                    if 0 <= j < CONV_K:
                        lo = b * span + CONV_SUB * a
                        acc = acc + sh_scr[lo:lo + rc, cs] * cw_ref[j:j + 1, cs]
            mu = jnp.mean(acc, axis=-1, keepdims=True)
            xc = acc - mu
            var = jnp.mean(xc * xc, axis=-1, keepdims=True)
            y = xc * lax.rsqrt(var + LN_EPS) * gg_ref[:, cs] + gb_ref[:, cs]
            y = _silu(y) * _silu(za_ref[r0:r0 + rc, cs].astype(F32))
            out_ref[r0:r0 + rc, cs] = y.astype(BF16)


def _conv_module(p, conv_w, conv_b, gn_g, gn_b, B, S, ts):
    T = B * S
    nt = S // ts
    hb = ts // HALO
    n_hblk = T // HALO

    def cur(col):
        return pl.BlockSpec((ts, W_A), lambda b, i: (b * nt + i, col))

    def prev(col):
        return pl.BlockSpec((HALO, W_A), lambda b, i: (jnp.maximum((b * nt + i) * hb - 1, 0), col))

    def nxt(col):
        return pl.BlockSpec((HALO, W_A), lambda b, i: (jnp.minimum((b * nt + i + 1) * hb, n_hblk - 1), col))

    def full(r):
        return pl.BlockSpec((r, W_A), lambda b, i: (0, 0))

    rc = min(ts, 128)
    shift = _conv_shift_matrix(rc)
    return pl.pallas_call(
        functools.partial(_conv_kernel, ts=ts, rc=rc),
        grid=(B, nt),
        in_specs=[cur(0), cur(1), cur(2), prev(0), prev(1), nxt(0), nxt(1),
                  full(CONV_K), full(1), full(1), full(1),
                  pl.BlockSpec(shift.shape, lambda b, i: (0, 0))],
        out_specs=pl.BlockSpec((ts, W_A), lambda b, i: (b * nt + i, 0)),
        out_shape=jax.ShapeDtypeStruct((T, W_A), BF16),
        scratch_shapes=[pltpu.VMEM((ts + 2 * HALO, W_A), BF16),
                        pltpu.VMEM((shift.shape[0], W_A), F32)],
        compiler_params=_params(("arbitrary", "arbitrary")),
        name="conv_module",
    )(p, p, p, p, p, p, p, conv_w, conv_b, gn_g, gn_b, shift)


def _mlstm_kernel(qtf_ref, kf_ref, vtf_ref, gcf_ref, grf_ref,
                  qtb_ref, kb_ref, vtb_ref, gcb_ref, grb_ref,
                  hf_ref, hb_ref, st_scr, m_scr, *, L):
    c = pl.program_id(1)

    @pl.when(c == 0)
    def _():
        st_scr[...] = jnp.zeros_like(st_scr)
        m_scr[...] = jnp.zeros_like(m_scr)

    row = lax.broadcasted_iota(jnp.int32, (L, L), 0)
    col = lax.broadcasted_iota(jnp.int32, (L, L), 1)
    lower = col <= row
    upper = col >= row
    lower_f = lower.astype(BF16)
    upper_f = upper.astype(BF16)
    scale = DQK_B ** -0.5
    ones_rows = (lax.broadcasted_iota(jnp.int32, (MLSTM_AUG - DV_B, L), 0) == 0).astype(BF16)

    streams = []
    for fwd, (qt_ref, k_ref, vt_ref, gc_ref, gr_ref, h_ref) in (
            (True, (qtf_ref, kf_ref, vtf_ref, gcf_ref, grf_ref, hf_ref)),
            (False, (qtb_ref, kb_ref, vtb_ref, gcb_ref, grb_ref, hb_ref))):
        gate_i = 0 if fwd else 2 * H_B
        gate_f = gate_i + H_B
        gc = gc_ref[...]
        gr = gr_ref[...]
        b_col_all = _tri_sum_left(lower_f if fwd else upper_f, _log_sigmoid(gc))
        b_row_all = _tri_sum_right(_log_sigmoid(gr), upper_f if fwd else lower_f)
        for h in range(H_B):
            b_col = b_col_all[:, gate_f + h:gate_f + h + 1]
            streams.append(dict(
                fwd=fwd, h=h, sidx=h if fwd else H_B + h, h_ref=h_ref, k_ref=k_ref, vt_ref=vt_ref,
                mask_t=upper if fwd else lower,
                b_col=b_col,
                b_row=b_row_all[gate_f + h:gate_f + h + 1, :],
                i_col=gc[:, gate_i + h:gate_i + h + 1],
                i_row=gr[gate_i + h:gate_i + h + 1, :],
                b_end=b_col[L - 1:L, :] if fwd else b_col[0:1, :],
                qt=(qt_ref[h * DQK_B:(h + 1) * DQK_B, :].astype(F32) * scale).astype(BF16)))

    def keys(s):
        return s["k_ref"][:, s["h"] * DQK_B:(s["h"] + 1) * DQK_B]

    def values(s):
        v = s["vt_ref"][s["h"] * DV_B:(s["h"] + 1) * DV_B, :]
        return jnp.concatenate([v, ones_rows], axis=0)

    for s in streams:
        s["qk"] = _dot(keys(s), s["qt"])
        s["inter"] = _dot(st_scr[s["sidx"]].astype(BF16), s["qt"])
    for s in streams:
        m = m_scr[s["sidx"]][:, 0:1]
        d_t = jnp.where(s["mask_t"], s["b_row"] + (s["i_col"] - s["b_col"]), -jnp.inf)
        a = s["b_row"] + m
        m_t = jnp.maximum(a, jnp.max(d_t, axis=0, keepdims=True))
        s_t = s["qk"] * jnp.exp(d_t - m_t)
        s["num"] = _dot(values(s), s_t.astype(BF16)) + jnp.exp(a - m_t) * s["inter"]
        s["m"], s["m_t"] = m, m_t
    for s in streams:
        den = s["num"][DV_B:DV_B + 1, :]
        hval = s["num"][0:DV_B, :] / jnp.maximum(jnp.abs(den), jnp.exp(-s["m_t"]))
        s["h_ref"][s["h"] * DV_B:(s["h"] + 1) * DV_B, :] = hval.astype(BF16)
    for s in streams:
        g_row = s["b_end"] - s["b_row"] + s["i_row"]
        m_new = jnp.maximum(s["b_end"] + s["m"], jnp.max(g_row, axis=1, keepdims=True))
        wk = jnp.exp(g_row - m_new)
        decay = jnp.exp(s["b_end"] + s["m"] - m_new)
        upd = _dot((values(s).astype(F32) * wk).astype(BF16), keys(s))
        st_scr[s["sidx"]] = decay * st_scr[s["sidx"]] + upd
        m_scr[s["sidx"]] = jnp.broadcast_to(m_new, (1, LANES))


def _mlstm(p, qt, vt, gates, gates_t, B, S, L):
    T = B * S
    nc = S // L

    def specs(chunk):
        return [
            pl.BlockSpec((H_B * DQK_B, L), lambda b, c: (0, b * nc + chunk(c))),
            pl.BlockSpec((L, H_B * DQK_B), lambda b, c: (b * nc + chunk(c), 0)),
            pl.BlockSpec((W_B, L), lambda b, c: (0, b * nc + chunk(c))),
            pl.BlockSpec((L, LANES), lambda b, c: (b * nc + chunk(c), 0)),
            pl.BlockSpec((4 * H_B, L), lambda b, c: (0, b * nc + chunk(c))),
        ]

    fwd = lambda c: c
    bwd = lambda c: nc - 1 - c
    n_streams = 2 * H_B
    return pl.pallas_call(
        functools.partial(_mlstm_kernel, L=L),
        grid=(B, nc),
        in_specs=specs(fwd) + specs(bwd),
        out_specs=[
            pl.BlockSpec((W_B, L), lambda b, c: (0, b * nc + c)),
            pl.BlockSpec((W_B, L), lambda b, c: (0, b * nc + nc - 1 - c)),
        ],
        out_shape=[jax.ShapeDtypeStruct((W_B, T), BF16)] * 2,
        scratch_shapes=[
            pltpu.VMEM((n_streams, MLSTM_AUG, DQK_B), F32),
            pltpu.VMEM((n_streams, 1, LANES), F32),
        ],
        compiler_params=_params(("arbitrary", "arbitrary")),
        name="mlstm",
    )(qt, p, vt, gates, gates_t, qt, p, vt, gates, gates_t)


def _even_out_kernel(ya_ref, hf_ref, hb_ref, o_ref, zb_ref, x_ref, hn_ref, w_ref, g_ref, out_ref):
    parts = []
    for h in range(H_B):
        cs = slice(h * DV_B, (h + 1) * DV_B)
        hm_t = hf_ref[cs, :].astype(F32) + hb_ref[cs, :].astype(F32)
        hm_t = hm_t * lax.rsqrt(jnp.mean(hm_t * hm_t, axis=0, keepdims=True) + RMS_EPS)
        hm = hm_t.T * hn_ref[:, cs]
        yb = _sigmoid(o_ref[:, cs].astype(F32)) * hm * _silu(zb_ref[:, cs].astype(F32))
        parts.append(yb.astype(BF16))
    y = _dot(ya_ref[...], w_ref[0:W_A, :])
    for h in range(H_B):
        y = y + _dot(parts[h], w_ref[W_A + h * DV_B:W_A + (h + 1) * DV_B, :])
    out_ref[...] = x_ref[...] + _rms(y, g_ref[...])


def _even_out(ya, hf, hb, p, x2, hn_g, w_out, g_post, tm):
    T = x2.shape[0]
    row = lambda w: pl.BlockSpec((tm, w), lambda i: (i, 0))
    feat = pl.BlockSpec((W_B, tm), lambda i: (0, i))
    return pl.pallas_call(
        _even_out_kernel,
        grid=(T // tm,),
        in_specs=[row(W_A), feat, feat,
                  pl.BlockSpec((tm, W_B), lambda i: (i, 0)),
                  pl.BlockSpec((tm, W_B), lambda i: (i, 1)),
                  row(D_MODEL),
                  pl.BlockSpec((1, W_B), lambda i: (0, 0)),
                  pl.BlockSpec((W_A + W_B, D_MODEL), lambda i: (0, 0)),
                  pl.BlockSpec((1, D_MODEL), lambda i: (0, 0))],
        out_specs=row(D_MODEL),
        out_shape=jax.ShapeDtypeStruct((T, D_MODEL), F32),
        compiler_params=_params(("arbitrary",)),
        name="even_out_proj",
    )(ya, hf, hb, p, p, x2, hn_g, w_out, g_post)


def _odd_in_kernel(x_ref, g_ref, wq_ref, wkv_ref, wf_ref, wzc_ref, wzd_ref, wkrt_ref,
                   gq_ref, wuqt_ref, gkv_ref, wuk_ref, wuvt_ref, cost_ref, sint_ref,
                   qt_ref, k_ref, vt_ref, f_ref, zc_ref, zd_ref):
    tm = x_ref.shape[0]
    h = _rms(x_ref[...], g_ref[...]).astype(BF16)
    half = QK_ROPE // 2
    ct = cost_ref[...]
    st = sint_ref[...]

    f_all = _dot(h, wf_ref[...]).astype(BF16)
    for g in range(D_GROUPS):
        f_ref[g] = f_all[:, g * D_GROUP_CH:(g + 1) * D_GROUP_CH]
    zc_ref[...] = _dot(h, wzc_ref[...]).astype(BF16)
    zd_ref[...] = _dot(h, wzd_ref[...]).astype(BF16)

    ckv = _rms(_dot(h, wkv_ref[...]), gkv_ref[...]).astype(BF16)
    lane = lax.broadcasted_iota(jnp.int32, (tm, LANES), 1)
    first_half = lane < QK_ROPE
    krt = _dot_nt(wkrt_ref[...], h)
    kr_rot = jnp.concatenate([krt[0:half] * ct - krt[half:QK_ROPE] * st,
                              krt[0:half] * st + krt[half:QK_ROPE] * ct], axis=0)
    kr = jnp.concatenate([kr_rot, kr_rot], axis=0).T
    kr_even = jnp.where(first_half, kr, 0.0).astype(BF16)
    kr_odd = jnp.where(first_half, 0.0, kr).astype(BF16)
    k_all = _dot(ckv, wuk_ref[...]).astype(BF16)
    for hd in range(H_C):
        k_ref[hd, :, 0:QK_NOPE] = k_all[:, hd * QK_NOPE:(hd + 1) * QK_NOPE]
        k_ref[hd, :, QK_NOPE:QK_PAD] = kr_even if hd % 2 == 0 else kr_odd

    vt = _dot_nt(wuvt_ref[...], ckv)
    ones_rows = (lax.broadcasted_iota(jnp.int32, (V_AUG - V_HEAD, tm), 0) == 0).astype(BF16)
    for hd in range(H_C):
        vt_ref[hd, 0:V_HEAD, :] = vt[hd * V_HEAD:(hd + 1) * V_HEAD, :].astype(BF16)
        vt_ref[hd, V_HEAD:V_AUG, :] = ones_rows

    scale = (QK_NOPE + QK_ROPE) ** -0.5 * math.log2(math.e)
    cq = _rms(_dot(h, wq_ref[...]), gq_ref[...]).astype(BF16)
    qn = _dot_nt(wuqt_ref[0:H_C * QK_NOPE, :], cq) * scale
    qr = _dot_nt(wuqt_ref[H_C * QK_NOPE:H_C * (QK_NOPE + QK_ROPE), :], cq) * scale
    zeros = jnp.zeros((QK_ROPE, tm), BF16)
    for hd in range(H_C):
        qt_ref[hd, 0:QK_NOPE, :] = qn[hd * QK_NOPE:(hd + 1) * QK_NOPE, :].astype(BF16)
        x1 = qr[hd * QK_ROPE:hd * QK_ROPE + half, :]
        x2 = qr[hd * QK_ROPE + half:(hd + 1) * QK_ROPE, :]
        lo = QK_NOPE if hd % 2 == 0 else QK_NOPE + QK_ROPE
        pad = QK_NOPE + QK_ROPE if hd % 2 == 0 else QK_NOPE
        qt_ref[hd, lo:lo + half, :] = (x1 * ct - x2 * st).astype(BF16)
        qt_ref[hd, lo + half:lo + QK_ROPE, :] = (x1 * st + x2 * ct).astype(BF16)
        qt_ref[hd, pad:pad + QK_ROPE, :] = zeros


def _odd_in(x2, g_pre, w_segs, g_q, w_uq_t, g_kv, w_uk, w_uv_t, cos_t, sin_t, B, S, tm):
    T = B * S
    nt = S // tm
    full = lambda a: pl.BlockSpec(a.shape, lambda i: (0,) * a.ndim)
    row = lambda w: pl.BlockSpec((tm, w), lambda i: (i, 0))
    return pl.pallas_call(
        _odd_in_kernel,
        grid=(T // tm,),
        in_specs=[row(D_MODEL), full(g_pre)] + [full(w) for w in w_segs]
                 + [full(g_q), full(w_uq_t), full(g_kv), full(w_uk), full(w_uv_t),
                    pl.BlockSpec((QK_ROPE // 2, tm), lambda i: (0, i)),
                    pl.BlockSpec((QK_ROPE // 2, tm), lambda i: (0, i))],
        out_specs=[pl.BlockSpec((None, H_C, QK_PAD, tm), lambda i: (i // nt, 0, 0, i % nt)),
                   pl.BlockSpec((None, H_C, tm, QK_PAD), lambda i: (i // nt, 0, i % nt, 0)),
                   pl.BlockSpec((None, H_C, None, V_AUG, tm), lambda i: (i // nt, 0, i % nt, 0, 0)),
                   pl.BlockSpec((D_GROUPS, None, tm, D_GROUP_CH), lambda i: (0, i // nt, i % nt, 0)),
                   row(W_C), row(W_D)],
        out_shape=[
            jax.ShapeDtypeStruct((B, H_C, QK_PAD, S), BF16),
            jax.ShapeDtypeStruct((B, H_C, S, QK_PAD), BF16),
            jax.ShapeDtypeStruct((B, H_C, nt, V_AUG, tm), BF16),
            jax.ShapeDtypeStruct((D_GROUPS, B, S, D_GROUP_CH), BF16),
            jax.ShapeDtypeStruct((T, W_C), BF16),
            jax.ShapeDtypeStruct((T, W_D), BF16),
        ],
        compiler_params=_params(("arbitrary",)),
        name="odd_in_proj",
    )(x2, g_pre, *w_segs, g_q, w_uq_t, g_kv, w_uk, w_uv_t, cos_t, sin_t)


def _attn_kernel(qt_ref, k_ref, vt_ref, o_ref, s_a, s_b, p_a, p_b, acc_scr, *, tk):
    nk = k_ref.shape[0] // tk
    qt = qt_ref[...]

    s_bufs = (s_a, s_b)
    p_bufs = (p_a, p_b)

    def scores(j):
        return _dot(k_ref[j * tk:(j + 1) * tk, :], qt)

    tv = vt_ref.shape[-1]

    def v_tile(j):
        return vt_ref[j * tk // tv][:, j * tk % tv:j * tk % tv + tk]

    s_bufs[0][...] = scores(0)
    m = None
    alpha_prev = None
    for j in range(nk):
        if j + 1 < nk:
            s_bufs[(j + 1) % 2][...] = scores(j + 1)
        s = s_bufs[j % 2][...]
        tile_max = jnp.max(s, axis=0, keepdims=True)
        m_new = tile_max if m is None else jnp.maximum(m, tile_max)
        if j >= 1:
            pv = _dot(v_tile(j - 1), p_bufs[(j - 1) % 2][...])
            acc_scr[...] = pv if j == 1 else alpha_prev * acc_scr[...] + pv
        alpha_prev = None if m is None else jnp.exp2(m - m_new)
        p_bufs[j % 2][...] = jnp.exp2((s - m_new).astype(BF16))
        m = m_new
    acc = alpha_prev * acc_scr[...] + _dot(v_tile(nk - 1), p_bufs[(nk - 1) % 2][...])
    o_ref[...] = (acc[0:V_HEAD, :] / acc[V_HEAD:V_HEAD + 1, :]).T.astype(BF16)


def _attention(qt, k, vt, B, S, tq, tk):
    nv, tv = vt.shape[2], vt.shape[4]
    assert S // tk >= 2 and tv % tk == 0
    return pl.pallas_call(
        functools.partial(_attn_kernel, tk=tk),
        grid=(B, H_C, S // tq),
        in_specs=[
            pl.BlockSpec((None, None, QK_PAD, tq), lambda b, h, i: (b, h, 0, i)),
            pl.BlockSpec((None, None, S, QK_PAD), lambda b, h, i: (b, h, 0, 0)),
            pl.BlockSpec((None, None, nv, V_AUG, tv), lambda b, h, i: (b, h, 0, 0, 0)),
        ],
        out_specs=pl.BlockSpec((None, tq, V_HEAD), lambda b, h, i: (b, i, h)),
        out_shape=jax.ShapeDtypeStruct((B, S, W_C), BF16),
        scratch_shapes=[pltpu.VMEM((tk, tq), F32), pltpu.VMEM((tk, tq), F32),
                        pltpu.VMEM((tk, tq), BF16), pltpu.VMEM((tk, tq), BF16),
                        pltpu.VMEM((V_AUG, tq), F32)],
        compiler_params=_params(("arbitrary", "arbitrary", "arbitrary")),
        name="flash_attention",
    )(qt, k, vt)


def _fft_tables(S):
    n1 = S // LANES
    a = np.arange(n1, dtype=np.float64)
    ang1 = 2.0 * np.pi * np.outer(a, a) / n1
    w1 = np.concatenate([np.cos(ang1), -np.sin(ang1)], axis=0)
    k1 = np.arange(n1)[:, None, None]
    k2 = np.arange(LANES)[None, :, None]
    n2 = np.arange(LANES)[None, None, :]
    kk = (k1 + n1 * k2) * n2 % S
    ang2 = 2.0 * np.pi * kk.astype(np.float64) / S
    gc = np.cos(ang2).reshape(n1 * LANES, LANES)
    gs = np.sin(ang2).reshape(n1 * LANES, LANES)
    c = np.arange(D_GROUP_CH, dtype=np.float64)
    angc = 2.0 * np.pi * np.outer(c, c) / D_GROUP_CH
    f32 = lambda t: jnp.asarray(t.astype(np.float32))
    return f32(w1), f32(gc), f32(gs), f32(np.cos(angc)), f32(np.sin(angc))


def _fft_pitch(n1):
    return n1 + 8


def _fft_kernel(x_ref, w1_ref, gc_ref, gs_ref, cc_ref, sc_ref, out_ref, ar_scr, ai_scr, z_scr, *, n1, cw):
    n_chunks = (LANES * LANES) // cw
    per = cw // LANES
    pitch = _fft_pitch(n1)
    w1 = w1_ref[...]

    def stage1(ch, carry):
        lo = pl.multiple_of(ch * cw, cw)
        res = _dot(w1, x_ref[:, pl.ds(lo, cw)])
        for j in range(per):
            r0 = pl.multiple_of((ch * per + j) * pitch, 8) if pitch % 8 == 0 else (ch * per + j) * pitch
            ar_scr[pl.ds(r0, n1), :] = res[0:n1, j * LANES:(j + 1) * LANES]
            ai_scr[pl.ds(r0, n1), :] = res[n1:2 * n1, j * LANES:(j + 1) * LANES]
        return carry

    lax.fori_loop(0, n_chunks, stage1, 0)

    ccm = cc_ref[...]
    scm = sc_ref[...]
    norm = 1.0 / math.sqrt(n1 * LANES * D_GROUP_CH)

    def sequence_dft(k1):
        ar = ar_scr[pl.ds(k1, LANES, stride=pitch), :]
        ai = ai_scr[pl.ds(k1, LANES, stride=pitch), :]
        a = jnp.concatenate([ar, ai], axis=1).astype(BF16)
        t0 = pl.multiple_of(k1 * LANES, LANES)
        p1 = _dot(gc_ref[pl.ds(t0, LANES), :], a)
        p2 = _dot(gs_ref[pl.ds(t0, LANES), :], a)
        zr = p1[:, 0:LANES] + p2[:, LANES:2 * LANES]
        zi = p1[:, LANES:2 * LANES] - p2[:, 0:LANES]
        return zr.astype(BF16), zi.astype(BF16)

    def channel_dft(k1, z):
        fr = _dot(z[0], ccm) + _dot(z[1], scm)
        z_scr[pl.ds(k1, LANES, stride=pitch), :] = fr * norm

    group = 8 if n1 % 8 == 0 else 1

    def stage2(i, carry):
        zs = [sequence_dft(i * group + u) for u in range(group)]
        for u in range(group):
            channel_dft(i * group + u, zs[u])
        return carry

    lax.fori_loop(0, n1 // group, stage2, 0)

    for k2 in range(LANES):
        out_ref[k2 * n1:(k2 + 1) * n1, :] = z_scr[k2 * pitch:k2 * pitch + n1, :]


def _fft(f4, B, S):
    n1 = S // LANES
    w1, gc, gs, cc, sc = _fft_tables(S)
    bf = lambda t: t.astype(BF16)
    x4 = f4.reshape(D_GROUPS, B, n1, LANES * D_GROUP_CH)
    cw = 2048
    full = lambda a: pl.BlockSpec(a.shape, lambda g, b: (0,) * a.ndim)
    tabs = [bf(w1), bf(gc), bf(gs), bf(cc), bf(sc)]
    return pl.pallas_call(
        functools.partial(_fft_kernel, n1=n1, cw=cw),
        grid=(D_GROUPS, B),
        in_specs=[pl.BlockSpec((None, None, n1, LANES * D_GROUP_CH), lambda g, b: (g, b, 0, 0))]
                 + [full(t) for t in tabs],
        out_specs=pl.BlockSpec((None, S, D_GROUP_CH), lambda g, b: (b, 0, g)),
        out_shape=jax.ShapeDtypeStruct((B, S, W_D), F32),
        scratch_shapes=[pltpu.VMEM((LANES * _fft_pitch(n1), D_GROUP_CH), F32)] * 3,
        compiler_params=_params(("arbitrary", "arbitrary")),
        name="fft2_real",
    )(x4, *tabs)


def _odd_out_kernel(att_ref, zc_ref, fr_ref, zd_ref, x_ref, wfd_ref, w_ref, g_ref, out_ref):
    yc = (att_ref[...].astype(F32) * _silu(zc_ref[...].astype(F32))).astype(BF16)
    yd = (_dot(fr_ref[...].astype(BF16), wfd_ref[...]) * _silu(zd_ref[...].astype(F32))).astype(BF16)
    y = _dot(yc, w_ref[0:W_C, :]) + _dot(yd, w_ref[W_C:W_C + W_D, :])
    out_ref[...] = x_ref[...] + _rms(y, g_ref[...])


def _odd_out(att, zc, fr, zd, x2, w_fd, w_out, g_post, tm):
    T = x2.shape[0]
    row = lambda w: pl.BlockSpec((tm, w), lambda i: (i, 0))
    full = lambda a: pl.BlockSpec(a.shape, lambda i: (0,) * a.ndim)
    return pl.pallas_call(
        _odd_out_kernel,
        grid=(T // tm,),
        in_specs=[row(W_C), row(W_C), row(W_D), row(W_D), row(D_MODEL), full(w_fd), full(w_out), full(g_post)],
        out_specs=row(D_MODEL),
        out_shape=jax.ShapeDtypeStruct((T, D_MODEL), F32),
        compiler_params=_params(("arbitrary",)),
        name="odd_out_proj",
    )(att, zc, fr, zd, x2, w_fd, w_out, g_post)


def _even_layer(x2, B, S, g_pre, g_post, w_in, b_gate, conv_w, conv_b, gn_g, gn_b, hn_g, w_out):
    o = np.cumsum([0, 3 * W_A, H_B * DQK_B, H_B * DQK_B, W_B, 2 * W_B, 4 * H_B])
    seg = lambda i: w_in[:, o[i]:o[i + 1]].astype(BF16)
    w_gate = jnp.pad(seg(5), ((0, 0), (0, LANES - 4 * H_B)))
    bg = jnp.pad(b_gate, (0, LANES - 4 * H_B)).reshape(1, LANES)
    tm = min(512, S)
    pa, poz, pk, qt, vt, gates, gates_t = _even_in(
        x2, g_pre.reshape(1, -1), seg(0), seg(4), seg(2), seg(1).T, seg(3).T, w_gate, bg,
        seg(5).T, b_gate.reshape(-1, 1), tm)
    ya = _conv_module(pa, conv_w, conv_b.reshape(1, -1), gn_g.reshape(1, -1), gn_b.reshape(1, -1),
                      B, S, min(512, S))
    hf, hb = _mlstm(pk, qt, vt, gates, gates_t, B, S, min(MLSTM_CHUNK, S))
    return _even_out(ya, hf, hb, poz, x2, hn_g.reshape(1, -1), w_out.astype(BF16), g_post.reshape(1, -1),
                     min(512, S))


def _odd_layer(x2, B, S, cos, sin, g_pre, g_post, w_in, g_q, w_uq, g_kv, w_ukv, w_fd, w_out):
    T = B * S
    o = np.cumsum([0, Q_LORA, KV_LORA, QK_ROPE, W_D, W_C, W_D])
    seg = lambda i: w_in[:, o[i]:o[i + 1]].astype(BF16)
    w_segs = [seg(0), seg(1), seg(3), seg(4), seg(5), seg(2).T]
    wq = w_uq.reshape(Q_LORA, H_C, QK_NOPE + QK_ROPE)
    w_uq_t = jnp.concatenate([wq[:, :, :QK_NOPE].reshape(Q_LORA, -1),
                              wq[:, :, QK_NOPE:].reshape(Q_LORA, -1)], axis=1).T.astype(BF16)
    wkv = w_ukv.reshape(KV_LORA, H_C, QK_NOPE + V_HEAD)
    w_uk = wkv[:, :, :QK_NOPE].reshape(KV_LORA, -1).astype(BF16)
    w_uv_t = wkv[:, :, QK_NOPE:].reshape(KV_LORA, -1).T.astype(BF16)
    tm = min(512, S // 2)
    qt, k, vt, f4, zc, zd = _odd_in(x2, g_pre.reshape(1, -1), w_segs, g_q.reshape(1, -1), w_uq_t,
                                    g_kv.reshape(1, -1), w_uk, w_uv_t, cos, sin, B, S, tm)
    att = _attention(qt, k, vt, B, S, min(ATTN_TQ, S), min(ATTN_TK, tm)).reshape(T, W_C)
    fr = _fft(f4, B, S).reshape(T, W_D)
    return _odd_out(att, zc, fr, zd, x2, w_fd.astype(BF16), w_out.astype(BF16), g_post.reshape(1, -1), tm)


def _rope_tables(positions):
    inv = ROPE_THETA ** (-jnp.arange(0, QK_ROPE, 2, dtype=F32) / QK_ROPE)
    ang = inv[:, None] * positions.astype(F32).reshape(1, -1)
    return jnp.cos(ang), jnp.sin(ang)


def kernel(x, positions, even_g_pre, even_g_post, even_w_in, even_b_gate, even_conv_w, even_conv_b,
           even_gn_g, even_gn_b, even_hn_g, even_w_out, odd_g_pre, odd_g_post, odd_w_in, odd_g_q,
           odd_w_uq, odd_g_kv, odd_w_ukv, odd_w_fd, odd_w_out):
    B, S, _ = x.shape
    cos, sin = _rope_tables(positions)
    h = x.reshape(B * S, D_MODEL)
    depth = even_w_in.shape[0] + odd_w_in.shape[0]
    for layer in range(depth):
        j = layer // 2
        if layer % 2 == 0:
            h = _even_layer(h, B, S, even_g_pre[j], even_g_post[j], even_w_in[j], even_b_gate[j],
                            even_conv_w[j], even_conv_b[j], even_gn_g[j], even_gn_b[j], even_hn_g[j],
                            even_w_out[j])
        else:
            h = _odd_layer(h, B, S, cos, sin, odd_g_pre[j], odd_g_post[j], odd_w_in[j], odd_g_q[j],
                           odd_w_uq[j], odd_g_kv[j], odd_w_ukv[j], odd_w_fd[j], odd_w_out[j])
    return h.reshape(B, S, D_MODEL)
```

```python
import functools
import math

import numpy as np
import jax
import jax.numpy as jnp
from jax import lax
from jax.experimental import pallas as pl
from jax.experimental.pallas import tpu as pltpu

F32 = jnp.float32
BF16 = jnp.bfloat16

D_MODEL = 1024
RMS_EPS = 1e-6
LN_EPS = 1e-5
W_A = 1024
CONV_K = 31
A_GROUPS = 8
H_B = 4
DQK_B = 128
DV_B = 256
W_B = 1024
H_C = 8
QK_NOPE = 128
QK_ROPE = 64
V_HEAD = 128
Q_LORA = 384
KV_LORA = 256
W_C = 1024
ROPE_THETA = 10000.0
D_GROUPS = 4
D_GROUP_CH = 128
W_D = 512

LANES = 128
HALO = 16
QK_PAD = 256
V_AUG = V_HEAD + 16
VMEM_LIMIT = 56 * 1024 * 1024

MLSTM_AUG = DV_B + 16

ROW_TILE = 512
CONV_TILE = 512
MLSTM_CHUNK = 256
ATTN_TQ = 2048
ATTN_TK = 256
FFT_LANE_CHUNK = 2048


def _params(sem, vmem=VMEM_LIMIT):
    return pltpu.CompilerParams(dimension_semantics=sem, vmem_limit_bytes=vmem)


def _sigmoid(x):
    return 1.0 / (1.0 + jnp.exp(-x))


def _silu(x):
    return x * _sigmoid(x)


def _log_sigmoid(x):
    return jnp.minimum(x, 0.0) - jnp.log(1.0 + jnp.exp(-jnp.abs(x)))


def _rms(x, g):
    return x * lax.rsqrt(jnp.mean(x * x, axis=-1, keepdims=True) + RMS_EPS) * g


def _dot(a, b):
    return jnp.dot(a, b, preferred_element_type=F32)


def _dot_nt(a, b):
    return lax.dot_general(a, b, (((1,), (1,)), ((), ())), preferred_element_type=F32)


def _split3(x):
    hi = x.astype(BF16)
    r1 = x - hi.astype(F32)
    mid = r1.astype(BF16)
    lo = (r1 - mid.astype(F32)).astype(BF16)
    return hi, mid, lo


def _tri_sum_left(tri, x):
    hi, mid, lo = _split3(x)
    return _dot(tri, lo) + _dot(tri, mid) + _dot(tri, hi)


def _tri_sum_right(x, tri):
    hi, mid, lo = _split3(x)
    return _dot(lo, tri) + _dot(mid, tri) + _dot(hi, tri)


def _even_in_kernel(x_ref, g_ref, wa_ref, woz_ref, wk_ref, wqt_ref, wvt_ref, wg_ref, bg_ref, wgt_ref, bgt_ref,
                    pa_ref, poz_ref, pk_ref, qt_ref, vt_ref, gates_ref, gates_t_ref):
    h = _rms(x_ref[...], g_ref[...]).astype(BF16)
    chunk = 1024
    for w_ref, p_ref in ((wa_ref, pa_ref), (woz_ref, poz_ref), (wk_ref, pk_ref)):
        for c0 in range(0, w_ref.shape[1], chunk):
            c1 = min(c0 + chunk, w_ref.shape[1])
            p_ref[:, c0:c1] = _dot(h, w_ref[:, c0:c1]).astype(BF16)
    qt_ref[...] = _dot_nt(wqt_ref[...], h).astype(BF16)
    vt_ref[...] = _dot_nt(wvt_ref[...], h).astype(BF16)
    gates_ref[...] = _dot(h, wg_ref[...]) + bg_ref[...]
    gates_t_ref[...] = _dot_nt(wgt_ref[...], h) + bgt_ref[...]


def _even_in(x2, g_pre, w_a, w_oz, w_k, w_qt, w_vt, w_gate, b_gate, w_gate_t, b_gate_t, tm):
    T = x2.shape[0]
    full = lambda a: pl.BlockSpec(a.shape, lambda i: (0, 0))
    row = lambda w: pl.BlockSpec((tm, w), lambda i: (i, 0))
    col = lambda r: pl.BlockSpec((r, tm), lambda i: (0, i))
    weights = (w_a, w_oz, w_k, w_qt, w_vt, w_gate, b_gate, w_gate_t, b_gate_t)
    return pl.pallas_call(
        _even_in_kernel,
        grid=(T // tm,),
        in_specs=[row(D_MODEL), full(g_pre)] + [full(w) for w in weights],
        out_specs=[row(w_a.shape[1]), row(w_oz.shape[1]), row(w_k.shape[1]),
                   col(H_B * DQK_B), col(W_B), row(LANES), col(4 * H_B)],
        out_shape=[
            jax.ShapeDtypeStruct((T, w_a.shape[1]), BF16),
            jax.ShapeDtypeStruct((T, w_oz.shape[1]), BF16),
            jax.ShapeDtypeStruct((T, w_k.shape[1]), BF16),
            jax.ShapeDtypeStruct((H_B * DQK_B, T), BF16),
            jax.ShapeDtypeStruct((W_B, T), BF16),
            jax.ShapeDtypeStruct((T, LANES), F32),
            jax.ShapeDtypeStruct((4 * H_B, T), F32),
        ],
        compiler_params=_params(("arbitrary",)),
        name="even_in_proj",
    )(x2, g_pre, *weights)


CONV_FIRST_TAP = HALO - CONV_K // 2
CONV_SUB = 8
CONV_TILE_TAPS = (CONV_FIRST_TAP + CONV_K - 1) // CONV_SUB + 1


def _conv_shift_matrix(rc):
    span = rc + (CONV_TILE_TAPS - 1) * CONV_SUB
    win = rc + 2 * HALO
    m = np.zeros((CONV_SUB * span, win), np.float32)
    for b in range(CONV_SUB):
        m[b * span + np.arange(span), np.arange(span) + b] = 1.0
    return jnp.asarray(m).astype(BF16)


def _conv_kernel(av_ref, ag_ref, za_ref, avp_ref, agp_ref, avn_ref, agn_ref,
                 cw_ref, cb_ref, gg_ref, gb_ref, sm_ref, out_ref, u_scr, sh_scr, *, ts, rc):
    i = pl.program_id(1)
    last = pl.num_programs(1) - 1

    def gated(a_ref, g_ref):
        return a_ref[...].astype(F32) * _sigmoid(g_ref[...].astype(F32))

    u_scr[HALO:HALO + ts, :] = gated(av_ref, ag_ref).astype(BF16)
    u_scr[0:HALO, :] = jnp.where(i > 0, gated(avp_ref, agp_ref), 0.0).astype(BF16)
    u_scr[HALO + ts:HALO + ts + HALO, :] = jnp.where(i < last, gated(avn_ref, agn_ref), 0.0).astype(BF16)

    span = rc + (CONV_TILE_TAPS - 1) * CONV_SUB
    for r0 in range(0, ts, rc):
        sh_scr[...] = _dot(sm_ref[...], u_scr[r0:r0 + rc + 2 * HALO, :])
        for g in range(A_GROUPS):
            cs = slice(g * LANES, (g + 1) * LANES)
            acc = jnp.zeros((rc, LANES), F32) + cb_ref[:, cs]
            for b in range(CONV_SUB):
                for a in range(CONV_TILE_TAPS):
                    j = CONV_SUB * a + b - CONV_FIRST_TAP
                    if 0 <= j < CONV_K:
                        lo = b * span + CONV_SUB * a
                        acc = acc + sh_scr[lo:lo + rc, cs] * cw_ref[j:j + 1, cs]
            mu = jnp.mean(acc, axis=-1, keepdims=True)
            xc = acc - mu
            var = jnp.mean(xc * xc, axis=-1, keepdims=True)
            y = xc * lax.rsqrt(var + LN_EPS) * gg_ref[:, cs] + gb_ref[:, cs]
            y = _silu(y) * _silu(za_ref[r0:r0 + rc, cs].astype(F32))
            out_ref[r0:r0 + rc, cs] = y.astype(BF16)


def _conv_module(p, conv_w, conv_b, gn_g, gn_b, B, S, ts):
    T = B * S
    nt = S // ts
    hb = ts // HALO
    n_hblk = T // HALO

    def cur(col):
        return pl.BlockSpec((ts, W_A), lambda b, i: (b * nt + i, col))

    def prev(col):
        return pl.BlockSpec((HALO, W_A), lambda b, i: (jnp.maximum((b * nt + i) * hb - 1, 0), col))

    def nxt(col):
        return pl.BlockSpec((HALO, W_A), lambda b, i: (jnp.minimum((b * nt + i + 1) * hb, n_hblk - 1), col))

    def full(r):
        return pl.BlockSpec((r, W_A), lambda b, i: (0, 0))

    rc = min(ts, 128)
    shift = _conv_shift_matrix(rc)
    return pl.pallas_call(
        functools.partial(_conv_kernel, ts=ts, rc=rc),
        grid=(B, nt),
        in_specs=[cur(0), cur(1), cur(2), prev(0), prev(1), nxt(0), nxt(1),
                  full(CONV_K), full(1), full(1), full(1),
                  pl.BlockSpec(shift.shape, lambda b, i: (0, 0))],
        out_specs=pl.BlockSpec((ts, W_A), lambda b, i: (b * nt + i, 0)),
        out_shape=jax.ShapeDtypeStruct((T, W_A), BF16),
        scratch_shapes=[pltpu.VMEM((ts + 2 * HALO, W_A), BF16),
                        pltpu.VMEM((shift.shape[0], W_A), F32)],
        compiler_params=_params(("arbitrary", "arbitrary")),
        name="conv_module",
    )(p, p, p, p, p, p, p, conv_w, conv_b, gn_g, gn_b, shift)


def _mlstm_kernel(qtf_ref, kf_ref, vtf_ref, gcf_ref, grf_ref,
                  qtb_ref, kb_ref, vtb_ref, gcb_ref, grb_ref,
                  hf_ref, hb_ref, st_scr, m_scr, *, L):
    c = pl.program_id(1)

    @pl.when(c == 0)
    def _():
        st_scr[...] = jnp.zeros_like(st_scr)
        m_scr[...] = jnp.zeros_like(m_scr)

    row = lax.broadcasted_iota(jnp.int32, (L, L), 0)
    col = lax.broadcasted_iota(jnp.int32, (L, L), 1)
    lower = col <= row
    upper = col >= row
    lower_f = lower.astype(BF16)
    upper_f = upper.astype(BF16)
    scale = DQK_B ** -0.5
    ones_rows = (lax.broadcasted_iota(jnp.int32, (MLSTM_AUG - DV_B, L), 0) == 0).astype(BF16)

    streams = []
    for fwd, (qt_ref, k_ref, vt_ref, gc_ref, gr_ref, h_ref) in (
            (True, (qtf_ref, kf_ref, vtf_ref, gcf_ref, grf_ref, hf_ref)),
            (False, (qtb_ref, kb_ref, vtb_ref, gcb_ref, grb_ref, hb_ref))):
        gate_i = 0 if fwd else 2 * H_B
        gate_f = gate_i + H_B
        gc = gc_ref[...]
        gr = gr_ref[...]
        b_col_all = _tri_sum_left(lower_f if fwd else upper_f, _log_sigmoid(gc))
        b_row_all = _tri_sum_right(_log_sigmoid(gr), upper_f if fwd else lower_f)
        for h in range(H_B):
            b_col = b_col_all[:, gate_f + h:gate_f + h + 1]
            streams.append(dict(
                fwd=fwd, h=h, sidx=h if fwd else H_B + h, h_ref=h_ref, k_ref=k_ref, vt_ref=vt_ref,
                mask_t=upper if fwd else lower,
                b_col=b_col,
                b_row=b_row_all[gate_f + h:gate_f + h + 1, :],
                i_col=gc[:, gate_i + h:gate_i + h + 1],
                i_row=gr[gate_i + h:gate_i + h + 1, :],
                b_end=b_col[L - 1:L, :] if fwd else b_col[0:1, :],
                qt=(qt_ref[h * DQK_B:(h + 1) * DQK_B, :].astype(F32) * scale).astype(BF16)))

    def keys(s):
        return s["k_ref"][:, s["h"] * DQK_B:(s["h"] + 1) * DQK_B]

    def values(s):
        v = s["vt_ref"][s["h"] * DV_B:(s["h"] + 1) * DV_B, :]
        return jnp.concatenate([v, ones_rows], axis=0)

    for s in streams:
        s["qk"] = _dot(keys(s), s["qt"])
        s["inter"] = _dot(st_scr[s["sidx"]].astype(BF16), s["qt"])
    for s in streams:
        m = m_scr[s["sidx"]][:, 0:1]
        d_t = jnp.where(s["mask_t"], s["b_row"] + (s["i_col"] - s["b_col"]), -jnp.inf)
        a = s["b_row"] + m
        m_t = jnp.maximum(a, jnp.max(d_t, axis=0, keepdims=True))
        s_t = s["qk"] * jnp.exp(d_t - m_t)
        s["num"] = _dot(values(s), s_t.astype(BF16)) + jnp.exp(a - m_t) * s["inter"]
        s["m"], s["m_t"] = m, m_t
    for s in streams:
        den = s["num"][DV_B:DV_B + 1, :]
        hval = s["num"][0:DV_B, :] / jnp.maximum(jnp.abs(den), jnp.exp(-s["m_t"]))
        s["h_ref"][s["h"] * DV_B:(s["h"] + 1) * DV_B, :] = hval.astype(BF16)
    for s in streams:
        g_row = s["b_end"] - s["b_row"] + s["i_row"]
        m_new = jnp.maximum(s["b_end"] + s["m"], jnp.max(g_row, axis=1, keepdims=True))
        wk = jnp.exp(g_row - m_new)
        decay = jnp.exp(s["b_end"] + s["m"] - m_new)
        upd = _dot((values(s).astype(F32) * wk).astype(BF16), keys(s))
        st_scr[s["sidx"]] = decay * st_scr[s["sidx"]] + upd
        m_scr[s["sidx"]] = jnp.broadcast_to(m_new, (1, LANES))


def _mlstm(p, qt, vt, gates, gates_t, B, S, L):
    T = B * S
    nc = S // L

    def specs(chunk):
        return [
            pl.BlockSpec((H_B * DQK_B, L), lambda b, c: (0, b * nc + chunk(c))),
            pl.BlockSpec((L, H_B * DQK_B), lambda b, c: (b * nc + chunk(c), 0)),
            pl.BlockSpec((W_B, L), lambda b, c: (0, b * nc + chunk(c))),
            pl.BlockSpec((L, LANES), lambda b, c: (b * nc + chunk(c), 0)),
            pl.BlockSpec((4 * H_B, L), lambda b, c: (0, b * nc + chunk(c))),
        ]

    fwd = lambda c: c
    bwd = lambda c: nc - 1 - c
    n_streams = 2 * H_B
    return pl.pallas_call(
        functools.partial(_mlstm_kernel, L=L),
        grid=(B, nc),
        in_specs=specs(fwd) + specs(bwd),
        out_specs=[
            pl.BlockSpec((W_B, L), lambda b, c: (0, b * nc + c)),
            pl.BlockSpec((W_B, L), lambda b, c: (0, b * nc + nc - 1 - c)),
        ],
        out_shape=[jax.ShapeDtypeStruct((W_B, T), BF16)] * 2,
        scratch_shapes=[
            pltpu.VMEM((n_streams, MLSTM_AUG, DQK_B), F32),
            pltpu.VMEM((n_streams, 1, LANES), F32),
        ],
        compiler_params=_params(("arbitrary", "arbitrary")),
        name="mlstm",
    )(qt, p, vt, gates, gates_t, qt, p, vt, gates, gates_t)


def _even_out_kernel(ya_ref, hf_ref, hb_ref, o_ref, zb_ref, x_ref, hn_ref, w_ref, g_ref, out_ref):
    parts = []
    for h in range(H_B):
        cs = slice(h * DV_B, (h + 1) * DV_B)
        hm_t = hf_ref[cs, :].astype(F32) + hb_ref[cs, :].astype(F32)
        hm_t = hm_t * lax.rsqrt(jnp.mean(hm_t * hm_t, axis=0, keepdims=True) + RMS_EPS)
        hm = hm_t.T * hn_ref[:, cs]
        yb = _sigmoid(o_ref[:, cs].astype(F32)) * hm * _silu(zb_ref[:, cs].astype(F32))
        parts.append(yb.astype(BF16))
    y = _dot(ya_ref[...], w_ref[0:W_A, :])
    for h in range(H_B):
        y = y + _dot(parts[h], w_ref[W_A + h * DV_B:W_A + (h + 1) * DV_B, :])
    out_ref[...] = x_ref[...] + _rms(y, g_ref[...])


def _even_out(ya, hf, hb, p, x2, hn_g, w_out, g_post, tm):
    T = x2.shape[0]
    row = lambda w: pl.BlockSpec((tm, w), lambda i: (i, 0))
    feat = pl.BlockSpec((W_B, tm), lambda i: (0, i))
    return pl.pallas_call(
        _even_out_kernel,
        grid=(T // tm,),
        in_specs=[row(W_A), feat, feat,
                  pl.BlockSpec((tm, W_B), lambda i: (i, 0)),
                  pl.BlockSpec((tm, W_B), lambda i: (i, 1)),
                  row(D_MODEL),
                  pl.BlockSpec((1, W_B), lambda i: (0, 0)),
                  pl.BlockSpec((W_A + W_B, D_MODEL), lambda i: (0, 0)),
                  pl.BlockSpec((1, D_MODEL), lambda i: (0, 0))],
        out_specs=row(D_MODEL),
        out_shape=jax.ShapeDtypeStruct((T, D_MODEL), F32),
        compiler_params=_params(("arbitrary",)),
        name="even_out_proj",
    )(ya, hf, hb, p, p, x2, hn_g, w_out, g_post)


def _odd_in_kernel(x_ref, g_ref, wq_ref, wkv_ref, wf_ref, wzc_ref, wzd_ref, wkrt_ref,
                   gq_ref, wuqt_ref, gkv_ref, wuk_ref, wuvt_ref, cost_ref, sint_ref,
                   qt_ref, k_ref, vt_ref, f_ref, zc_ref, zd_ref):
    tm = x_ref.shape[0]
    h = _rms(x_ref[...], g_ref[...]).astype(BF16)
    half = QK_ROPE // 2
    ct = cost_ref[...]
    st = sint_ref[...]

    f_all = _dot(h, wf_ref[...]).astype(BF16)
    for g in range(D_GROUPS):
        f_ref[g] = f_all[:, g * D_GROUP_CH:(g + 1) * D_GROUP_CH]
    zc_ref[...] = _dot(h, wzc_ref[...]).astype(BF16)
    zd_ref[...] = _dot(h, wzd_ref[...]).astype(BF16)

    ckv = _rms(_dot(h, wkv_ref[...]), gkv_ref[...]).astype(BF16)
    lane = lax.broadcasted_iota(jnp.int32, (tm, LANES), 1)
    first_half = lane < QK_ROPE
    krt = _dot_nt(wkrt_ref[...], h)
    kr_rot = jnp.concatenate([krt[0:half] * ct - krt[half:QK_ROPE] * st,
                              krt[0:half] * st + krt[half:QK_ROPE] * ct], axis=0)
    kr = jnp.concatenate([kr_rot, kr_rot], axis=0).T
    kr_even = jnp.where(first_half, kr, 0.0).astype(BF16)
    kr_odd = jnp.where(first_half, 0.0, kr).astype(BF16)
    k_all = _dot(ckv, wuk_ref[...]).astype(BF16)
    for hd in range(H_C):
        k_ref[hd, :, 0:QK_NOPE] = k_all[:, hd * QK_NOPE:(hd + 1) * QK_NOPE]
        k_ref[hd, :, QK_NOPE:QK_PAD] = kr_even if hd % 2 == 0 else kr_odd

    vt = _dot_nt(wuvt_ref[...], ckv)
    ones_rows = (lax.broadcasted_iota(jnp.int32, (V_AUG - V_HEAD, tm), 0) == 0).astype(BF16)
    for hd in range(H_C):
        vt_ref[hd, 0:V_HEAD, :] = vt[hd * V_HEAD:(hd + 1) * V_HEAD, :].astype(BF16)
        vt_ref[hd, V_HEAD:V_AUG, :] = ones_rows

    scale = (QK_NOPE + QK_ROPE) ** -0.5 * math.log2(math.e)
    cq = _rms(_dot(h, wq_ref[...]), gq_ref[...]).astype(BF16)
    qn = _dot_nt(wuqt_ref[0:H_C * QK_NOPE, :], cq) * scale
    qr = _dot_nt(wuqt_ref[H_C * QK_NOPE:H_C * (QK_NOPE + QK_ROPE), :], cq) * scale
    zeros = jnp.zeros((QK_ROPE, tm), BF16)
    for hd in range(H_C):
        qt_ref[hd, 0:QK_NOPE, :] = qn[hd * QK_NOPE:(hd + 1) * QK_NOPE, :].astype(BF16)
        x1 = qr[hd * QK_ROPE:hd * QK_ROPE + half, :]
        x2 = qr[hd * QK_ROPE + half:(hd + 1) * QK_ROPE, :]
        lo = QK_NOPE if hd % 2 == 0 else QK_NOPE + QK_ROPE
        pad = QK_NOPE + QK_ROPE if hd % 2 == 0 else QK_NOPE
        qt_ref[hd, lo:lo + half, :] = (x1 * ct - x2 * st).astype(BF16)
        qt_ref[hd, lo + half:lo + QK_ROPE, :] = (x1 * st + x2 * ct).astype(BF16)
        qt_ref[hd, pad:pad + QK_ROPE, :] = zeros


def _odd_in(x2, g_pre, w_segs, g_q, w_uq_t, g_kv, w_uk, w_uv_t, cos_t, sin_t, B, S, tm):
    T = B * S
    nt = S // tm
    full = lambda a: pl.BlockSpec(a.shape, lambda i: (0,) * a.ndim)
    row = lambda w: pl.BlockSpec((tm, w), lambda i: (i, 0))
    return pl.pallas_call(
        _odd_in_kernel,
        grid=(T // tm,),
        in_specs=[row(D_MODEL), full(g_pre)] + [full(w) for w in w_segs]
                 + [full(g_q), full(w_uq_t), full(g_kv), full(w_uk), full(w_uv_t),
                    pl.BlockSpec((QK_ROPE // 2, tm), lambda i: (0, i)),
                    pl.BlockSpec((QK_ROPE // 2, tm), lambda i: (0, i))],
        out_specs=[pl.BlockSpec((None, H_C, QK_PAD, tm), lambda i: (i // nt, 0, 0, i % nt)),
                   pl.BlockSpec((None, H_C, tm, QK_PAD), lambda i: (i // nt, 0, i % nt, 0)),
                   pl.BlockSpec((None, H_C, None, V_AUG, tm), lambda i: (i // nt, 0, i % nt, 0, 0)),
                   pl.BlockSpec((D_GROUPS, None, tm, D_GROUP_CH), lambda i: (0, i // nt, i % nt, 0)),
                   row(W_C), row(W_D)],
        out_shape=[
            jax.ShapeDtypeStruct((B, H_C, QK_PAD, S), BF16),
            jax.ShapeDtypeStruct((B, H_C, S, QK_PAD), BF16),
            jax.ShapeDtypeStruct((B, H_C, nt, V_AUG, tm), BF16),
            jax.ShapeDtypeStruct((D_GROUPS, B, S, D_GROUP_CH), BF16),
            jax.ShapeDtypeStruct((T, W_C), BF16),
            jax.ShapeDtypeStruct((T, W_D), BF16),
        ],
        compiler_params=_params(("arbitrary",)),
        name="odd_in_proj",
    )(x2, g_pre, *w_segs, g_q, w_uq_t, g_kv, w_uk, w_uv_t, cos_t, sin_t)


def _attn_kernel(qt_ref, k_ref, vt_ref, o_ref, s_a, s_b, p_a, p_b, acc_scr, *, tk):
    nk = k_ref.shape[0] // tk
    qt = qt_ref[...]

    s_bufs = (s_a, s_b)
    p_bufs = (p_a, p_b)

    def scores(j):
        return _dot(k_ref[j * tk:(j + 1) * tk, :], qt)

    tv = vt_ref.shape[-1]

    def v_tile(j):
        return vt_ref[j * tk // tv][:, j * tk % tv:j * tk % tv + tk]

    s_bufs[0][...] = scores(0)
    m = None
    alpha_prev = None
    for j in range(nk):
        if j + 1 < nk:
            s_bufs[(j + 1) % 2][...] = scores(j + 1)
        s = s_bufs[j % 2][...]
        tile_max = jnp.max(s, axis=0, keepdims=True)
        m_new = tile_max if m is None else jnp.maximum(m, tile_max)
        if j >= 1:
            pv = _dot(v_tile(j - 1), p_bufs[(j - 1) % 2][...])
            acc_scr[...] = pv if j == 1 else alpha_prev * acc_scr[...] + pv
        alpha_prev = None if m is None else jnp.exp2(m - m_new)
        p_bufs[j % 2][...] = jnp.exp2((s - m_new).astype(BF16))
        m = m_new
    acc = alpha_prev * acc_scr[...] + _dot(v_tile(nk - 1), p_bufs[(nk - 1) % 2][...])
    o_ref[...] = (acc[0:V_HEAD, :] / acc[V_HEAD:V_HEAD + 1, :]).T.astype(BF16)


def _attention(qt, k, vt, B, S, tq, tk):
    nv, tv = vt.shape[2], vt.shape[4]
    assert S // tk >= 2 and tv % tk == 0
    return pl.pallas_call(
        functools.partial(_attn_kernel, tk=tk),
        grid=(B, H_C, S // tq),
        in_specs=[
            pl.BlockSpec((None, None, QK_PAD, tq), lambda b, h, i: (b, h, 0, i)),
            pl.BlockSpec((None, None, S, QK_PAD), lambda b, h, i: (b, h, 0, 0)),
            pl.BlockSpec((None, None, nv, V_AUG, tv), lambda b, h, i: (b, h, 0, 0, 0)),
        ],
        out_specs=pl.BlockSpec((None, tq, V_HEAD), lambda b, h, i: (b, i, h)),
        out_shape=jax.ShapeDtypeStruct((B, S, W_C), BF16),
        scratch_shapes=[pltpu.VMEM((tk, tq), F32), pltpu.VMEM((tk, tq), F32),
                        pltpu.VMEM((tk, tq), BF16), pltpu.VMEM((tk, tq), BF16),
                        pltpu.VMEM((V_AUG, tq), F32)],
        compiler_params=_params(("arbitrary", "arbitrary", "arbitrary")),
        name="flash_attention",
    )(qt, k, vt)


def _fft_tables(S):
    n1 = S // LANES
    a = np.arange(n1, dtype=np.float64)
    ang1 = 2.0 * np.pi * np.outer(a, a) / n1
    w1 = np.concatenate([np.cos(ang1), -np.sin(ang1)], axis=0)
    k1 = np.arange(n1)[:, None, None]
    k2 = np.arange(LANES)[None, :, None]
    n2 = np.arange(LANES)[None, None, :]
    kk = (k1 + n1 * k2) * n2 % S
    ang2 = 2.0 * np.pi * kk.astype(np.float64) / S
    gc = np.cos(ang2).reshape(n1 * LANES, LANES)
    gs = np.sin(ang2).reshape(n1 * LANES, LANES)
    c = np.arange(D_GROUP_CH, dtype=np.float64)
    angc = 2.0 * np.pi * np.outer(c, c) / D_GROUP_CH
    f32 = lambda t: jnp.asarray(t.astype(np.float32))
    return f32(w1), f32(gc), f32(gs), f32(np.cos(angc)), f32(np.sin(angc))


def _fft_pitch(n1):
    return n1 + 8


def _fft_kernel(x_ref, w1_ref, gc_ref, gs_ref, cc_ref, sc_ref, out_ref, ar_scr, ai_scr, z_scr, *, n1, cw):
    n_chunks = (LANES * LANES) // cw
    per = cw // LANES
    pitch = _fft_pitch(n1)
    w1 = w1_ref[...]

    def stage1(ch, carry):
        lo = pl.multiple_of(ch * cw, cw)
        res = _dot(w1, x_ref[:, pl.ds(lo, cw)])
        for j in range(per):
            r0 = pl.multiple_of((ch * per + j) * pitch, 8) if pitch % 8 == 0 else (ch * per + j) * pitch
            ar_scr[pl.ds(r0, n1), :] = res[0:n1, j * LANES:(j + 1) * LANES]
            ai_scr[pl.ds(r0, n1), :] = res[n1:2 * n1, j * LANES:(j + 1) * LANES]
        return carry

    lax.fori_loop(0, n_chunks, stage1, 0)

    ccm = cc_ref[...]
    scm = sc_ref[...]
    norm = 1.0 / math.sqrt(n1 * LANES * D_GROUP_CH)

    def sequence_dft(k1):
        ar = ar_scr[pl.ds(k1, LANES, stride=pitch), :]
        ai = ai_scr[pl.ds(k1, LANES, stride=pitch), :]
        a = jnp.concatenate([ar, ai], axis=1).astype(BF16)
        t0 = pl.multiple_of(k1 * LANES, LANES)
        p1 = _dot(gc_ref[pl.ds(t0, LANES), :], a)
        p2 = _dot(gs_ref[pl.ds(t0, LANES), :], a)
        zr = p1[:, 0:LANES] + p2[:, LANES:2 * LANES]
        zi = p1[:, LANES:2 * LANES] - p2[:, 0:LANES]
        return zr.astype(BF16), zi.astype(BF16)

    def channel_dft(k1, z):
        fr = _dot(z[0], ccm) + _dot(z[1], scm)
        z_scr[pl.ds(k1, LANES, stride=pitch), :] = fr * norm

    group = 8 if n1 % 8 == 0 else 1

    def stage2(i, carry):
        zs = [sequence_dft(i * group + u) for u in range(group)]
        for u in range(group):
            channel_dft(i * group + u, zs[u])
        return carry

    lax.fori_loop(0, n1 // group, stage2, 0)

    for k2 in range(LANES):
        out_ref[k2 * n1:(k2 + 1) * n1, :] = z_scr[k2 * pitch:k2 * pitch + n1, :]


def _fft(f4, B, S):
    n1 = S // LANES
    w1, gc, gs, cc, sc = _fft_tables(S)
    bf = lambda t: t.astype(BF16)
    x4 = f4.reshape(D_GROUPS, B, n1, LANES * D_GROUP_CH)
    cw = FFT_LANE_CHUNK
    full = lambda a: pl.BlockSpec(a.shape, lambda g, b: (0,) * a.ndim)
    tabs = [bf(w1), bf(gc), bf(gs), bf(cc), bf(sc)]
    return pl.pallas_call(
        functools.partial(_fft_kernel, n1=n1, cw=cw),
        grid=(D_GROUPS, B),
        in_specs=[pl.BlockSpec((None, None, n1, LANES * D_GROUP_CH), lambda g, b: (g, b, 0, 0))]
                 + [full(t) for t in tabs],
        out_specs=pl.BlockSpec((None, S, D_GROUP_CH), lambda g, b: (b, 0, g)),
        out_shape=jax.ShapeDtypeStruct((B, S, W_D), F32),
        scratch_shapes=[pltpu.VMEM((LANES * _fft_pitch(n1), D_GROUP_CH), F32)] * 3,
        compiler_params=_params(("arbitrary", "arbitrary")),
        name="fft2_real",
    )(x4, *tabs)


def _odd_out_kernel(att_ref, zc_ref, fr_ref, zd_ref, x_ref, wfd_ref, w_ref, g_ref, out_ref):
    yc = (att_ref[...].astype(F32) * _silu(zc_ref[...].astype(F32))).astype(BF16)
    yd = (_dot(fr_ref[...].astype(BF16), wfd_ref[...]) * _silu(zd_ref[...].astype(F32))).astype(BF16)
    y = _dot(yc, w_ref[0:W_C, :]) + _dot(yd, w_ref[W_C:W_C + W_D, :])
    out_ref[...] = x_ref[...] + _rms(y, g_ref[...])


def _odd_out(att, zc, fr, zd, x2, w_fd, w_out, g_post, tm):
    T = x2.shape[0]
    row = lambda w: pl.BlockSpec((tm, w), lambda i: (i, 0))
    full = lambda a: pl.BlockSpec(a.shape, lambda i: (0,) * a.ndim)
    return pl.pallas_call(
        _odd_out_kernel,
        grid=(T // tm,),
        in_specs=[row(W_C), row(W_C), row(W_D), row(W_D), row(D_MODEL), full(w_fd), full(w_out), full(g_post)],
        out_specs=row(D_MODEL),
        out_shape=jax.ShapeDtypeStruct((T, D_MODEL), F32),
        compiler_params=_params(("arbitrary",)),
        name="odd_out_proj",
    )(att, zc, fr, zd, x2, w_fd, w_out, g_post)


def _even_layer(x2, B, S, g_pre, g_post, w_in, b_gate, conv_w, conv_b, gn_g, gn_b, hn_g, w_out):
    o = np.cumsum([0, 3 * W_A, H_B * DQK_B, H_B * DQK_B, W_B, 2 * W_B, 4 * H_B])
    seg = lambda i: w_in[:, o[i]:o[i + 1]].astype(BF16)
    w_gate = jnp.pad(seg(5), ((0, 0), (0, LANES - 4 * H_B)))
    bg = jnp.pad(b_gate, (0, LANES - 4 * H_B)).reshape(1, LANES)
    tm = min(ROW_TILE, S)
    pa, poz, pk, qt, vt, gates, gates_t = _even_in(
        x2, g_pre.reshape(1, -1), seg(0), seg(4), seg(2), seg(1).T, seg(3).T, w_gate, bg,
        seg(5).T, b_gate.reshape(-1, 1), tm)
    ya = _conv_module(pa, conv_w, conv_b.reshape(1, -1), gn_g.reshape(1, -1), gn_b.reshape(1, -1),
                      B, S, min(CONV_TILE, S))
    hf, hb = _mlstm(pk, qt, vt, gates, gates_t, B, S, min(MLSTM_CHUNK, S))
    return _even_out(ya, hf, hb, poz, x2, hn_g.reshape(1, -1), w_out.astype(BF16), g_post.reshape(1, -1), tm)


def _odd_layer(x2, B, S, cos, sin, g_pre, g_post, w_in, g_q, w_uq, g_kv, w_ukv, w_fd, w_out):
    T = B * S
    o = np.cumsum([0, Q_LORA, KV_LORA, QK_ROPE, W_D, W_C, W_D])
    seg = lambda i: w_in[:, o[i]:o[i + 1]].astype(BF16)
    w_segs = [seg(0), seg(1), seg(3), seg(4), seg(5), seg(2).T]
    wq = w_uq.reshape(Q_LORA, H_C, QK_NOPE + QK_ROPE)
    w_uq_t = jnp.concatenate([wq[:, :, :QK_NOPE].reshape(Q_LORA, -1),
                              wq[:, :, QK_NOPE:].reshape(Q_LORA, -1)], axis=1).T.astype(BF16)
    wkv = w_ukv.reshape(KV_LORA, H_C, QK_NOPE + V_HEAD)
    w_uk = wkv[:, :, :QK_NOPE].reshape(KV_LORA, -1).astype(BF16)
    w_uv_t = wkv[:, :, QK_NOPE:].reshape(KV_LORA, -1).T.astype(BF16)
    tm = min(ROW_TILE, S // 2)
    qt, k, vt, f4, zc, zd = _odd_in(x2, g_pre.reshape(1, -1), w_segs, g_q.reshape(1, -1), w_uq_t,
                                    g_kv.reshape(1, -1), w_uk, w_uv_t, cos, sin, B, S, tm)
    att = _attention(qt, k, vt, B, S, min(ATTN_TQ, S), min(ATTN_TK, tm)).reshape(T, W_C)
    fr = _fft(f4, B, S).reshape(T, W_D)
    return _odd_out(att, zc, fr, zd, x2, w_fd.astype(BF16), w_out.astype(BF16), g_post.reshape(1, -1), tm)


def _rope_tables(positions):
    inv = ROPE_THETA ** (-jnp.arange(0, QK_ROPE, 2, dtype=F32) / QK_ROPE)
    ang = inv[:, None] * positions.astype(F32).reshape(1, -1)
    return jnp.cos(ang), jnp.sin(ang)


def kernel(x, positions, even_g_pre, even_g_post, even_w_in, even_b_gate, even_conv_w, even_conv_b,
           even_gn_g, even_gn_b, even_hn_g, even_w_out, odd_g_pre, odd_g_post, odd_w_in, odd_g_q,
           odd_w_uq, odd_g_kv, odd_w_ukv, odd_w_fd, odd_w_out):
    B, S, _ = x.shape
    cos, sin = _rope_tables(positions)
    h = x.reshape(B * S, D_MODEL)
    depth = even_w_in.shape[0] + odd_w_in.shape[0]
    for layer in range(depth):
        j = layer // 2
        if layer % 2 == 0:
            h = _even_layer(h, B, S, even_g_pre[j], even_g_post[j], even_w_in[j], even_b_gate[j],
                            even_conv_w[j], even_conv_b[j], even_gn_g[j], even_gn_b[j], even_hn_g[j],
                            even_w_out[j])
        else:
            h = _odd_layer(h, B, S, cos, sin, odd_g_pre[j], odd_g_post[j], odd_w_in[j], odd_g_q[j],
                           odd_w_uq[j], odd_g_kv[j], odd_w_ukv[j], odd_w_fd[j], odd_w_out[j])
    return h.reshape(B, S, D_MODEL)
```

```python
import functools
import math

import numpy as np
import jax
import jax.numpy as jnp
from jax import lax
from jax.experimental import pallas as pl
from jax.experimental.pallas import tpu as pltpu

F32 = jnp.float32
BF16 = jnp.bfloat16

D_MODEL = 1024
RMS_EPS = 1e-6
LN_EPS = 1e-5
W_A = 1024
CONV_K = 31
A_GROUPS = 8
H_B = 4
DQK_B = 128
DV_B = 256
W_B = 1024
H_C = 8
QK_NOPE = 128
QK_ROPE = 64
V_HEAD = 128
Q_LORA = 384
KV_LORA = 256
W_C = 1024
ROPE_THETA = 10000.0
D_GROUPS = 4
D_GROUP_CH = 128
W_D = 512

LANES = 128
HALO = 16
QK_PAD = 256
V_AUG = V_HEAD + 16
VMEM_LIMIT = 56 * 1024 * 1024

MLSTM_AUG = DV_B + 16

ROW_TILE = 512
CONV_TILE = 512
MLSTM_CHUNK = 256
ATTN_TQ = 2048
ATTN_TK = 256
FFT_LANE_CHUNK = 2048


def _params(sem, vmem=VMEM_LIMIT):
    return pltpu.CompilerParams(dimension_semantics=sem, vmem_limit_bytes=vmem)


def _sigmoid(x):
    return 1.0 / (1.0 + jnp.exp(-x))


def _silu(x):
    return x * _sigmoid(x)


def _log_sigmoid(x):
    return jnp.minimum(x, 0.0) - jnp.log(1.0 + jnp.exp(-jnp.abs(x)))


def _rms(x, g):
    return x * lax.rsqrt(jnp.mean(x * x, axis=-1, keepdims=True) + RMS_EPS) * g


def _dot(a, b):
    return jnp.dot(a, b, preferred_element_type=F32)


def _dot_nt(a, b):
    return lax.dot_general(a, b, (((1,), (1,)), ((), ())), preferred_element_type=F32)


def _split3(x):
    hi = x.astype(BF16)
    r1 = x - hi.astype(F32)
    mid = r1.astype(BF16)
    lo = (r1 - mid.astype(F32)).astype(BF16)
    return hi, mid, lo


def _tri_sum_left(tri, x):
    hi, mid, lo = _split3(x)
    return _dot(tri, lo) + _dot(tri, mid) + _dot(tri, hi)


def _tri_sum_right(x, tri):
    hi, mid, lo = _split3(x)
    return _dot(lo, tri) + _dot(mid, tri) + _dot(hi, tri)


def _even_in_kernel(x_ref, g_ref, wa_ref, woz_ref, wk_ref, wqt_ref, wvt_ref, wg_ref, bg_ref, wgt_ref, bgt_ref,
                    pa_ref, poz_ref, pk_ref, qt_ref, vt_ref, gates_ref, gates_t_ref):
    h = _rms(x_ref[...], g_ref[...]).astype(BF16)
    chunk = 1024
    for w_ref, p_ref in ((wa_ref, pa_ref), (woz_ref, poz_ref), (wk_ref, pk_ref)):
        for c0 in range(0, w_ref.shape[1], chunk):
            c1 = min(c0 + chunk, w_ref.shape[1])
            p_ref[:, c0:c1] = _dot(h, w_ref[:, c0:c1]).astype(BF16)
    qt_ref[...] = _dot_nt(wqt_ref[...], h).astype(BF16)
    vt_ref[...] = _dot_nt(wvt_ref[...], h).astype(BF16)
    gates_ref[...] = _dot(h, wg_ref[...]) + bg_ref[...]
    gates_t_ref[...] = _dot_nt(wgt_ref[...], h) + bgt_ref[...]


def _even_in(x2, g_pre, w_a, w_oz, w_k, w_qt, w_vt, w_gate, b_gate, w_gate_t, b_gate_t, tm):
    T = x2.shape[0]
    full = lambda a: pl.BlockSpec(a.shape, lambda i: (0, 0))
    row = lambda w: pl.BlockSpec((tm, w), lambda i: (i, 0))
    col = lambda r: pl.BlockSpec((r, tm), lambda i: (0, i))
    weights = (w_a, w_oz, w_k, w_qt, w_vt, w_gate, b_gate, w_gate_t, b_gate_t)
    return pl.pallas_call(
        _even_in_kernel,
        grid=(T // tm,),
        in_specs=[row(D_MODEL), full(g_pre)] + [full(w) for w in weights],
        out_specs=[row(w_a.shape[1]), row(w_oz.shape[1]), row(w_k.shape[1]),
                   col(H_B * DQK_B), col(W_B), row(LANES), col(4 * H_B)],
        out_shape=[
            jax.ShapeDtypeStruct((T, w_a.shape[1]), BF16),
            jax.ShapeDtypeStruct((T, w_oz.shape[1]), BF16),
            jax.ShapeDtypeStruct((T, w_k.shape[1]), BF16),
            jax.ShapeDtypeStruct((H_B * DQK_B, T), BF16),
            jax.ShapeDtypeStruct((W_B, T), BF16),
            jax.ShapeDtypeStruct((T, LANES), F32),
            jax.ShapeDtypeStruct((4 * H_B, T), F32),
        ],
        compiler_params=_params(("arbitrary",)),
        name="even_in_proj",
    )(x2, g_pre, *weights)


CONV_FIRST_TAP = HALO - CONV_K // 2
CONV_SUB = 8
CONV_TILE_TAPS = (CONV_FIRST_TAP + CONV_K - 1) // CONV_SUB + 1


def _conv_shift_matrix(rc):
    span = rc + (CONV_TILE_TAPS - 1) * CONV_SUB
    win = rc + 2 * HALO
    m = np.zeros((CONV_SUB * span, win), np.float32)
    for b in range(CONV_SUB):
        m[b * span + np.arange(span), np.arange(span) + b] = 1.0
    return jnp.asarray(m).astype(BF16)


def _conv_kernel(av_ref, ag_ref, za_ref, avp_ref, agp_ref, avn_ref, agn_ref,
                 cw_ref, cb_ref, gg_ref, gb_ref, sm_ref, out_ref, u_scr, sh_scr, *, ts, rc):
    i = pl.program_id(1)
    last = pl.num_programs(1) - 1

    def gated(a_ref, g_ref):
        return a_ref[...].astype(F32) * _sigmoid(g_ref[...].astype(F32))

    u_scr[HALO:HALO + ts, :] = gated(av_ref, ag_ref).astype(BF16)
    u_scr[0:HALO, :] = jnp.where(i > 0, gated(avp_ref, agp_ref), 0.0).astype(BF16)
    u_scr[HALO + ts:HALO + ts + HALO, :] = jnp.where(i < last, gated(avn_ref, agn_ref), 0.0).astype(BF16)

    span = rc + (CONV_TILE_TAPS - 1) * CONV_SUB
    for r0 in range(0, ts, rc):
        sh_scr[...] = _dot(sm_ref[...], u_scr[r0:r0 + rc + 2 * HALO, :])
        for g in range(A_GROUPS):
            cs = slice(g * LANES, (g + 1) * LANES)
            acc = jnp.zeros((rc, LANES), F32) + cb_ref[:, cs]
            for b in range(CONV_SUB):
                for a in range(CONV_TILE_TAPS):
                    j = CONV_SUB * a + b - CONV_FIRST_TAP
                    if 0 <= j < CONV_K:
                        lo = b * span + CONV_SUB * a
                        acc = acc + sh_scr[lo:lo + rc, cs] * cw_ref[j:j + 1, cs]
            mu = jnp.mean(acc, axis=-1, keepdims=True)
            xc = acc - mu
            var = jnp.mean(xc * xc, axis=-1, keepdims=True)
            y = xc * lax.rsqrt(var + LN_EPS) * gg_ref[:, cs] + gb_ref[:, cs]
            y = _silu(y) * _silu(za_ref[r0:r0 + rc, cs].astype(F32))
            out_ref[r0:r0 + rc, cs] = y.astype(BF16)


def _conv_module(p, conv_w, conv_b, gn_g, gn_b, B, S, ts):
    T = B * S
    nt = S // ts
    hb = ts // HALO
    n_hblk = T // HALO

    def cur(col):
        return pl.BlockSpec((ts, W_A), lambda b, i: (b * nt + i, col))

    def prev(col):
        return pl.BlockSpec((HALO, W_A), lambda b, i: (jnp.maximum((b * nt + i) * hb - 1, 0), col))

    def nxt(col):
        return pl.BlockSpec((HALO, W_A), lambda b, i: (jnp.minimum((b * nt + i + 1) * hb, n_hblk - 1), col))

    def full(r):
        return pl.BlockSpec((r, W_A), lambda b, i: (0, 0))

    rc = min(ts, 128)
    shift = _conv_shift_matrix(rc)
    return pl.pallas_call(
        functools.partial(_conv_kernel, ts=ts, rc=rc),
        grid=(B, nt),
        in_specs=[cur(0), cur(1), cur(2), prev(0), prev(1), nxt(0), nxt(1),
                  full(CONV_K), full(1), full(1), full(1),
                  pl.BlockSpec(shift.shape, lambda b, i: (0, 0))],
        out_specs=pl.BlockSpec((ts, W_A), lambda b, i: (b * nt + i, 0)),
        out_shape=jax.ShapeDtypeStruct((T, W_A), BF16),
        scratch_shapes=[pltpu.VMEM((ts + 2 * HALO, W_A), BF16),
                        pltpu.VMEM((shift.shape[0], W_A), F32)],
        compiler_params=_params(("arbitrary", "arbitrary")),
        name="conv_module",
    )(p, p, p, p, p, p, p, conv_w, conv_b, gn_g, gn_b, shift)


def _mlstm_kernel(qtf_ref, kf_ref, vtf_ref, gcf_ref, grf_ref,
                  qtb_ref, kb_ref, vtb_ref, gcb_ref, grb_ref,
                  hf_ref, hb_ref, st_scr, m_scr, *, L):
    c = pl.program_id(1)

    @pl.when(c == 0)
    def _():
        st_scr[...] = jnp.zeros_like(st_scr)
        m_scr[...] = jnp.zeros_like(m_scr)

    row = lax.broadcasted_iota(jnp.int32, (L, L), 0)
    col = lax.broadcasted_iota(jnp.int32, (L, L), 1)
    lower = col <= row
    upper = col >= row
    lower_f = lower.astype(BF16)
    upper_f = upper.astype(BF16)
    scale = DQK_B ** -0.5
    ones_rows = (lax.broadcasted_iota(jnp.int32, (MLSTM_AUG - DV_B, L), 0) == 0).astype(BF16)

    streams = []
    for fwd, (qt_ref, k_ref, vt_ref, gc_ref, gr_ref, h_ref) in (
            (True, (qtf_ref, kf_ref, vtf_ref, gcf_ref, grf_ref, hf_ref)),
            (False, (qtb_ref, kb_ref, vtb_ref, gcb_ref, grb_ref, hb_ref))):
        gate_i = 0 if fwd else 2 * H_B
        gate_f = gate_i + H_B
        gc = gc_ref[...]
        gr = gr_ref[...]
        b_col_all = _tri_sum_left(lower_f if fwd else upper_f, _log_sigmoid(gc))
        b_row_all = _tri_sum_right(_log_sigmoid(gr), upper_f if fwd else lower_f)
        for h in range(H_B):
            b_col = b_col_all[:, gate_f + h:gate_f + h + 1]
            streams.append(dict(
                fwd=fwd, h=h, sidx=h if fwd else H_B + h, h_ref=h_ref, k_ref=k_ref, vt_ref=vt_ref,
                mask_t=upper if fwd else lower,
                b_col=b_col,
                b_row=b_row_all[gate_f + h:gate_f + h + 1, :],
                i_col=gc[:, gate_i + h:gate_i + h + 1],
                i_row=gr[gate_i + h:gate_i + h + 1, :],
                b_end=b_col[L - 1:L, :] if fwd else b_col[0:1, :],
                qt=(qt_ref[h * DQK_B:(h + 1) * DQK_B, :].astype(F32) * scale).astype(BF16)))

    def keys(s):
        return s["k_ref"][:, s["h"] * DQK_B:(s["h"] + 1) * DQK_B]

    def values(s):
        v = s["vt_ref"][s["h"] * DV_B:(s["h"] + 1) * DV_B, :]
        return jnp.concatenate([v, ones_rows], axis=0)

    for s in streams:
        s["qk"] = _dot(keys(s), s["qt"])
        s["inter"] = _dot(st_scr[s["sidx"]].astype(BF16), s["qt"])
    for s in streams:
        m = m_scr[s["sidx"]][:, 0:1]
        d_t = jnp.where(s["mask_t"], s["b_row"] + (s["i_col"] - s["b_col"]), -jnp.inf)
        a = s["b_row"] + m
        m_t = jnp.maximum(a, jnp.max(d_t, axis=0, keepdims=True))
        s_t = s["qk"] * jnp.exp(d_t - m_t)
        s["num"] = _dot(values(s), s_t.astype(BF16)) + jnp.exp(a - m_t) * s["inter"]
        s["m"], s["m_t"] = m, m_t
    for s in streams:
        den = s["num"][DV_B:DV_B + 1, :]
        hval = s["num"][0:DV_B, :] / jnp.maximum(jnp.abs(den), jnp.exp(-s["m_t"]))
        s["h_ref"][s["h"] * DV_B:(s["h"] + 1) * DV_B, :] = hval.astype(BF16)
    for s in streams:
        g_row = s["b_end"] - s["b_row"] + s["i_row"]
        m_new = jnp.maximum(s["b_end"] + s["m"], jnp.max(g_row, axis=1, keepdims=True))
        wk = jnp.exp(g_row - m_new)
        decay = jnp.exp(s["b_end"] + s["m"] - m_new)
        upd = _dot((values(s).astype(F32) * wk).astype(BF16), keys(s))
        st_scr[s["sidx"]] = decay * st_scr[s["sidx"]] + upd
        m_scr[s["sidx"]] = jnp.broadcast_to(m_new, (1, LANES))


def _mlstm(p, qt, vt, gates, gates_t, B, S, L):
    T = B * S
    nc = S // L

    def specs(chunk):
        return [
            pl.BlockSpec((H_B * DQK_B, L), lambda b, c: (0, b * nc + chunk(c))),
            pl.BlockSpec((L, H_B * DQK_B), lambda b, c: (b * nc + chunk(c), 0)),
            pl.BlockSpec((W_B, L), lambda b, c: (0, b * nc + chunk(c))),
            pl.BlockSpec((L, LANES), lambda b, c: (b * nc + chunk(c), 0)),
            pl.BlockSpec((4 * H_B, L), lambda b, c: (0, b * nc + chunk(c))),
        ]

    fwd = lambda c: c
    bwd = lambda c: nc - 1 - c
    n_streams = 2 * H_B
    return pl.pallas_call(
        functools.partial(_mlstm_kernel, L=L),
        grid=(B, nc),
        in_specs=specs(fwd) + specs(bwd),
        out_specs=[
            pl.BlockSpec((W_B, L), lambda b, c: (0, b * nc + c)),
            pl.BlockSpec((W_B, L), lambda b, c: (0, b * nc + nc - 1 - c)),
        ],
        out_shape=[jax.ShapeDtypeStruct((W_B, T), BF16)] * 2,
        scratch_shapes=[
            pltpu.VMEM((n_streams, MLSTM_AUG, DQK_B), F32),
            pltpu.VMEM((n_streams, 1, LANES), F32),
        ],
        compiler_params=_params(("arbitrary", "arbitrary")),
        name="mlstm",
    )(qt, p, vt, gates, gates_t, qt, p, vt, gates, gates_t)


def _even_out_kernel(ya_ref, hf_ref, hb_ref, o_ref, zb_ref, x_ref, hn_ref, w_ref, g_ref, out_ref):
    parts = []
    for h in range(H_B):
        cs = slice(h * DV_B, (h + 1) * DV_B)
        hm_t = hf_ref[cs, :].astype(F32) + hb_ref[cs, :].astype(F32)
        hm_t = hm_t * lax.rsqrt(jnp.mean(hm_t * hm_t, axis=0, keepdims=True) + RMS_EPS)
        hm = hm_t.T * hn_ref[:, cs]
        yb = _sigmoid(o_ref[:, cs].astype(F32)) * hm * _silu(zb_ref[:, cs].astype(F32))
        parts.append(yb.astype(BF16))
    y = _dot(ya_ref[...], w_ref[0:W_A, :])
    for h in range(H_B):
        y = y + _dot(parts[h], w_ref[W_A + h * DV_B:W_A + (h + 1) * DV_B, :])
    out_ref[...] = x_ref[...] + _rms(y, g_ref[...])


def _even_out(ya, hf, hb, p, x2, hn_g, w_out, g_post, tm):
    T = x2.shape[0]
    row = lambda w: pl.BlockSpec((tm, w), lambda i: (i, 0))
    feat = pl.BlockSpec((W_B, tm), lambda i: (0, i))
    return pl.pallas_call(
        _even_out_kernel,
        grid=(T // tm,),
        in_specs=[row(W_A), feat, feat,
                  pl.BlockSpec((tm, W_B), lambda i: (i, 0)),
                  pl.BlockSpec((tm, W_B), lambda i: (i, 1)),
                  row(D_MODEL),
                  pl.BlockSpec((1, W_B), lambda i: (0, 0)),
                  pl.BlockSpec((W_A + W_B, D_MODEL), lambda i: (0, 0)),
                  pl.BlockSpec((1, D_MODEL), lambda i: (0, 0))],
        out_specs=row(D_MODEL),
        out_shape=jax.ShapeDtypeStruct((T, D_MODEL), F32),
        compiler_params=_params(("arbitrary",)),
        name="even_out_proj",
    )(ya, hf, hb, p, p, x2, hn_g, w_out, g_post)


def _odd_in_kernel(x_ref, g_ref, wq_ref, wkv_ref, wf_ref, wzc_ref, wzd_ref, wkrt_ref,
                   gq_ref, wuqt_ref, gkv_ref, wuk_ref, wuvt_ref, cost_ref, sint_ref,
                   qt_ref, k_ref, vt_ref, f_ref, zc_ref, zd_ref):
    tm = x_ref.shape[0]
    h = _rms(x_ref[...], g_ref[...]).astype(BF16)
    half = QK_ROPE // 2
    ct = cost_ref[...]
    st = sint_ref[...]

    f_all = _dot(h, wf_ref[...]).astype(BF16)
    for g in range(D_GROUPS):
        f_ref[g] = f_all[:, g * D_GROUP_CH:(g + 1) * D_GROUP_CH]
    zc_ref[...] = _dot(h, wzc_ref[...]).astype(BF16)
    zd_ref[...] = _dot(h, wzd_ref[...]).astype(BF16)

    ckv = _rms(_dot(h, wkv_ref[...]), gkv_ref[...]).astype(BF16)
    lane = lax.broadcasted_iota(jnp.int32, (tm, LANES), 1)
    first_half = lane < QK_ROPE
    krt = _dot_nt(wkrt_ref[...], h)
    kr_rot = jnp.concatenate([krt[0:half] * ct - krt[half:QK_ROPE] * st,
                              krt[0:half] * st + krt[half:QK_ROPE] * ct], axis=0)
    kr = jnp.concatenate([kr_rot, kr_rot], axis=0).T
    kr_even = jnp.where(first_half, kr, 0.0).astype(BF16)
    kr_odd = jnp.where(first_half, 0.0, kr).astype(BF16)
    k_all = _dot(ckv, wuk_ref[...]).astype(BF16)
    for hd in range(H_C):
        k_ref[hd, :, 0:QK_NOPE] = k_all[:, hd * QK_NOPE:(hd + 1) * QK_NOPE]
        k_ref[hd, :, QK_NOPE:QK_PAD] = kr_even if hd % 2 == 0 else kr_odd

    vt = _dot_nt(wuvt_ref[...], ckv)
    ones_rows = (lax.broadcasted_iota(jnp.int32, (V_AUG - V_HEAD, tm), 0) == 0).astype(BF16)
    for hd in range(H_C):
        vt_ref[hd, 0:V_HEAD, :] = vt[hd * V_HEAD:(hd + 1) * V_HEAD, :].astype(BF16)
        vt_ref[hd, V_HEAD:V_AUG, :] = ones_rows

    scale = (QK_NOPE + QK_ROPE) ** -0.5 * math.log2(math.e)
    cq = _rms(_dot(h, wq_ref[...]), gq_ref[...]).astype(BF16)
    qn = _dot_nt(wuqt_ref[0:H_C * QK_NOPE, :], cq) * scale
    qr = _dot_nt(wuqt_ref[H_C * QK_NOPE:H_C * (QK_NOPE + QK_ROPE), :], cq) * scale
    zeros = jnp.zeros((QK_ROPE, tm), BF16)
    for hd in range(H_C):
        qt_ref[hd, 0:QK_NOPE, :] = qn[hd * QK_NOPE:(hd + 1) * QK_NOPE, :].astype(BF16)
        x1 = qr[hd * QK_ROPE:hd * QK_ROPE + half, :]
        x2 = qr[hd * QK_ROPE + half:(hd + 1) * QK_ROPE, :]
        lo = QK_NOPE if hd % 2 == 0 else QK_NOPE + QK_ROPE
        pad = QK_NOPE + QK_ROPE if hd % 2 == 0 else QK_NOPE
        qt_ref[hd, lo:lo + half, :] = (x1 * ct - x2 * st).astype(BF16)
        qt_ref[hd, lo + half:lo + QK_ROPE, :] = (x1 * st + x2 * ct).astype(BF16)
        qt_ref[hd, pad:pad + QK_ROPE, :] = zeros


def _odd_in(x2, g_pre, w_segs, g_q, w_uq_t, g_kv, w_uk, w_uv_t, cos_t, sin_t, B, S, tm):
    T = B * S
    nt = S // tm
    full = lambda a: pl.BlockSpec(a.shape, lambda i: (0,) * a.ndim)
    row = lambda w: pl.BlockSpec((tm, w), lambda i: (i, 0))
    return pl.pallas_call(
        _odd_in_kernel,
        grid=(T // tm,),
        in_specs=[row(D_MODEL), full(g_pre)] + [full(w) for w in w_segs]
                 + [full(g_q), full(w_uq_t), full(g_kv), full(w_uk), full(w_uv_t),
                    pl.BlockSpec((QK_ROPE // 2, tm), lambda i: (0, i)),
                    pl.BlockSpec((QK_ROPE // 2, tm), lambda i: (0, i))],
        out_specs=[pl.BlockSpec((None, H_C, QK_PAD, tm), lambda i: (i // nt, 0, 0, i % nt)),
                   pl.BlockSpec((None, H_C, tm, QK_PAD), lambda i: (i // nt, 0, i % nt, 0)),
                   pl.BlockSpec((None, H_C, None, V_AUG, tm), lambda i: (i // nt, 0, i % nt, 0, 0)),
                   pl.BlockSpec((D_GROUPS, None, tm, D_GROUP_CH), lambda i: (0, i // nt, i % nt, 0)),
                   row(W_C), row(W_D)],
        out_shape=[
            jax.ShapeDtypeStruct((B, H_C, QK_PAD, S), BF16),
            jax.ShapeDtypeStruct((B, H_C, S, QK_PAD), BF16),
            jax.ShapeDtypeStruct((B, H_C, nt, V_AUG, tm), BF16),
            jax.ShapeDtypeStruct((D_GROUPS, B, S, D_GROUP_CH), BF16),
            jax.ShapeDtypeStruct((T, W_C), BF16),
            jax.ShapeDtypeStruct((T, W_D), BF16),
        ],
        compiler_params=_params(("arbitrary",)),
        name="odd_in_proj",
    )(x2, g_pre, *w_segs, g_q, w_uq_t, g_kv, w_uk, w_uv_t, cos_t, sin_t)


def _attn_kernel(qt_ref, k_ref, vt_ref, o_ref, s_a, s_b, acc_scr, *, tk):
    nk = k_ref.shape[0] // tk
    qt = qt_ref[...]

    s_bufs = (s_a, s_b)

    def scores(j):
        return _dot(k_ref[j * tk:(j + 1) * tk, :], qt)

    tv = vt_ref.shape[-1]

    def v_tile(j):
        return vt_ref[j * tk // tv][:, j * tk % tv:j * tk % tv + tk]

    s_bufs[0][...] = scores(0)
    m = None
    for j in range(nk):
        if j + 1 < nk:
            s_bufs[(j + 1) % 2][...] = scores(j + 1)
        s = s_bufs[j % 2][...]
        tile_max = jnp.max(s, axis=0, keepdims=True)
        m_new = tile_max if m is None else jnp.maximum(m, tile_max)
        pv = _dot(v_tile(j), jnp.exp2((s - m_new).astype(BF16)))
        acc_scr[...] = pv if m is None else jnp.exp2(m - m_new) * acc_scr[...] + pv
        m = m_new
    acc = acc_scr[...]
    o_ref[...] = (acc[0:V_HEAD, :] / acc[V_HEAD:V_HEAD + 1, :]).T.astype(BF16)


def _attention(qt, k, vt, B, S, tq, tk):
    nv, tv = vt.shape[2], vt.shape[4]
    assert S // tk >= 2 and tv % tk == 0
    return pl.pallas_call(
        functools.partial(_attn_kernel, tk=tk),
        grid=(B, H_C, S // tq),
        in_specs=[
            pl.BlockSpec((None, None, QK_PAD, tq), lambda b, h, i: (b, h, 0, i)),
            pl.BlockSpec((None, None, S, QK_PAD), lambda b, h, i: (b, h, 0, 0)),
            pl.BlockSpec((None, None, nv, V_AUG, tv), lambda b, h, i: (b, h, 0, 0, 0)),
        ],
        out_specs=pl.BlockSpec((None, tq, V_HEAD), lambda b, h, i: (b, i, h)),
        out_shape=jax.ShapeDtypeStruct((B, S, W_C), BF16),
        scratch_shapes=[pltpu.VMEM((tk, tq), F32), pltpu.VMEM((tk, tq), F32),
                        pltpu.VMEM((V_AUG, tq), F32)],
        compiler_params=_params(("arbitrary", "arbitrary", "arbitrary")),
        name="flash_attention",
    )(qt, k, vt)


def _fft_tables(S):
    n1 = S // LANES
    a = np.arange(n1, dtype=np.float64)
    ang1 = 2.0 * np.pi * np.outer(a, a) / n1
    w1 = np.concatenate([np.cos(ang1), -np.sin(ang1)], axis=0)
    k1 = np.arange(n1)[:, None, None]
    k2 = np.arange(LANES)[None, :, None]
    n2 = np.arange(LANES)[None, None, :]
    kk = (k1 + n1 * k2) * n2 % S
    ang2 = 2.0 * np.pi * kk.astype(np.float64) / S
    gc = np.cos(ang2).reshape(n1 * LANES, LANES)
    gs = np.sin(ang2).reshape(n1 * LANES, LANES)
    c = np.arange(D_GROUP_CH, dtype=np.float64)
    angc = 2.0 * np.pi * np.outer(c, c) / D_GROUP_CH
    f32 = lambda t: jnp.asarray(t.astype(np.float32))
    return f32(w1), f32(gc), f32(gs), f32(np.cos(angc)), f32(np.sin(angc))


def _fft_pitch(n1):
    return n1 + 8


def _fft_kernel(x_ref, w1_ref, gc_ref, gs_ref, cc_ref, sc_ref, out_ref, ar_scr, ai_scr, z_scr, *, n1, cw):
    n_chunks = (LANES * LANES) // cw
    per = cw // LANES
    pitch = _fft_pitch(n1)
    w1 = w1_ref[...]

    def stage1(ch, carry):
        lo = pl.multiple_of(ch * cw, cw)
        res = _dot(w1, x_ref[:, pl.ds(lo, cw)])
        for j in range(per):
            r0 = pl.multiple_of((ch * per + j) * pitch, 8) if pitch % 8 == 0 else (ch * per + j) * pitch
            ar_scr[pl.ds(r0, n1), :] = res[0:n1, j * LANES:(j + 1) * LANES]
            ai_scr[pl.ds(r0, n1), :] = res[n1:2 * n1, j * LANES:(j + 1) * LANES]
        return carry

    lax.fori_loop(0, n_chunks, stage1, 0)

    ccm = cc_ref[...]
    scm = sc_ref[...]
    norm = 1.0 / math.sqrt(n1 * LANES * D_GROUP_CH)

    def sequence_dft(k1):
        ar = ar_scr[pl.ds(k1, LANES, stride=pitch), :]
        ai = ai_scr[pl.ds(k1, LANES, stride=pitch), :]
        a = jnp.concatenate([ar, ai], axis=1).astype(BF16)
        t0 = pl.multiple_of(k1 * LANES, LANES)
        p1 = _dot(gc_ref[pl.ds(t0, LANES), :], a)
        p2 = _dot(gs_ref[pl.ds(t0, LANES), :], a)
        zr = p1[:, 0:LANES] + p2[:, LANES:2 * LANES]
        zi = p1[:, LANES:2 * LANES] - p2[:, 0:LANES]
        return zr.astype(BF16), zi.astype(BF16)

    def channel_dft(k1, z):
        fr = _dot(z[0], ccm) + _dot(z[1], scm)
        z_scr[pl.ds(k1, LANES, stride=pitch), :] = fr * norm

    group = 8 if n1 % 8 == 0 else 1

    def stage2(i, carry):
        zs = [sequence_dft(i * group + u) for u in range(group)]
        for u in range(group):
            channel_dft(i * group + u, zs[u])
        return carry

    lax.fori_loop(0, n1 // group, stage2, 0)

    for k2 in range(LANES):
        out_ref[k2 * n1:(k2 + 1) * n1, :] = z_scr[k2 * pitch:k2 * pitch + n1, :]


def _fft(f4, B, S):
    n1 = S // LANES
    w1, gc, gs, cc, sc = _fft_tables(S)
    bf = lambda t: t.astype(BF16)
    x4 = f4.reshape(D_GROUPS, B, n1, LANES * D_GROUP_CH)
    cw = FFT_LANE_CHUNK
    full = lambda a: pl.BlockSpec(a.shape, lambda g, b: (0,) * a.ndim)
    tabs = [bf(w1), bf(gc), bf(gs), bf(cc), bf(sc)]
    return pl.pallas_call(
        functools.partial(_fft_kernel, n1=n1, cw=cw),
        grid=(D_GROUPS, B),
        in_specs=[pl.BlockSpec((None, None, n1, LANES * D_GROUP_CH), lambda g, b: (g, b, 0, 0))]
                 + [full(t) for t in tabs],
        out_specs=pl.BlockSpec((None, S, D_GROUP_CH), lambda g, b: (b, 0, g)),
        out_shape=jax.ShapeDtypeStruct((B, S, W_D), F32),
        scratch_shapes=[pltpu.VMEM((LANES * _fft_pitch(n1), D_GROUP_CH), F32)] * 3,
        compiler_params=_params(("arbitrary", "arbitrary")),
        name="fft2_real",
    )(x4, *tabs)


def _odd_out_kernel(att_ref, zc_ref, fr_ref, zd_ref, x_ref, wfd_ref, w_ref, g_ref, out_ref):
    yc = (att_ref[...].astype(F32) * _silu(zc_ref[...].astype(F32))).astype(BF16)
    yd = (_dot(fr_ref[...].astype(BF16), wfd_ref[...]) * _silu(zd_ref[...].astype(F32))).astype(BF16)
    y = _dot(yc, w_ref[0:W_C, :]) + _dot(yd, w_ref[W_C:W_C + W_D, :])
    out_ref[...] = x_ref[...] + _rms(y, g_ref[...])


def _odd_out(att, zc, fr, zd, x2, w_fd, w_out, g_post, tm):
    T = x2.shape[0]
    row = lambda w: pl.BlockSpec((tm, w), lambda i: (i, 0))
    full = lambda a: pl.BlockSpec(a.shape, lambda i: (0,) * a.ndim)
    return pl.pallas_call(
        _odd_out_kernel,
        grid=(T // tm,),
        in_specs=[row(W_C), row(W_C), row(W_D), row(W_D), row(D_MODEL), full(w_fd), full(w_out), full(g_post)],
        out_specs=row(D_MODEL),
        out_shape=jax.ShapeDtypeStruct((T, D_MODEL), F32),
        compiler_params=_params(("arbitrary",)),
        name="odd_out_proj",
    )(att, zc, fr, zd, x2, w_fd, w_out, g_post)


def _even_layer(x2, B, S, g_pre, g_post, w_in, b_gate, conv_w, conv_b, gn_g, gn_b, hn_g, w_out):
    o = np.cumsum([0, 3 * W_A, H_B * DQK_B, H_B * DQK_B, W_B, 2 * W_B, 4 * H_B])
    seg = lambda i: w_in[:, o[i]:o[i + 1]].astype(BF16)
    w_gate = jnp.pad(seg(5), ((0, 0), (0, LANES - 4 * H_B)))
    bg = jnp.pad(b_gate, (0, LANES - 4 * H_B)).reshape(1, LANES)
    tm = min(ROW_TILE, S)
    pa, poz, pk, qt, vt, gates, gates_t = _even_in(
        x2, g_pre.reshape(1, -1), seg(0), seg(4), seg(2), seg(1).T, seg(3).T, w_gate, bg,
        seg(5).T, b_gate.reshape(-1, 1), tm)
    ya = _conv_module(pa, conv_w, conv_b.reshape(1, -1), gn_g.reshape(1, -1), gn_b.reshape(1, -1),
                      B, S, min(CONV_TILE, S))
    hf, hb = _mlstm(pk, qt, vt, gates, gates_t, B, S, min(MLSTM_CHUNK, S))
    return _even_out(ya, hf, hb, poz, x2, hn_g.reshape(1, -1), w_out.astype(BF16), g_post.reshape(1, -1), tm)


def _odd_layer(x2, B, S, cos, sin, g_pre, g_post, w_in, g_q, w_uq, g_kv, w_ukv, w_fd, w_out):
    T = B * S
    o = np.cumsum([0, Q_LORA, KV_LORA, QK_ROPE, W_D, W_C, W_D])
    seg = lambda i: w_in[:, o[i]:o[i + 1]].astype(BF16)
    w_segs = [seg(0), seg(1), seg(3), seg(4), seg(5), seg(2).T]
    wq = w_uq.reshape(Q_LORA, H_C, QK_NOPE + QK_ROPE)
    w_uq_t = jnp.concatenate([wq[:, :, :QK_NOPE].reshape(Q_LORA, -1),
                              wq[:, :, QK_NOPE:].reshape(Q_LORA, -1)], axis=1).T.astype(BF16)
    wkv = w_ukv.reshape(KV_LORA, H_C, QK_NOPE + V_HEAD)
    w_uk = wkv[:, :, :QK_NOPE].reshape(KV_LORA, -1).astype(BF16)
    w_uv_t = wkv[:, :, QK_NOPE:].reshape(KV_LORA, -1).T.astype(BF16)
    tm = min(ROW_TILE, S // 2)
    qt, k, vt, f4, zc, zd = _odd_in(x2, g_pre.reshape(1, -1), w_segs, g_q.reshape(1, -1), w_uq_t,
                                    g_kv.reshape(1, -1), w_uk, w_uv_t, cos, sin, B, S, tm)
    att = _attention(qt, k, vt, B, S, min(ATTN_TQ, S), min(ATTN_TK, tm)).reshape(T, W_C)
    fr = _fft(f4, B, S).reshape(T, W_D)
    return _odd_out(att, zc, fr, zd, x2, w_fd.astype(BF16), w_out.astype(BF16), g_post.reshape(1, -1), tm)


def _rope_tables(positions):
    inv = ROPE_THETA ** (-jnp.arange(0, QK_ROPE, 2, dtype=F32) / QK_ROPE)
    ang = inv[:, None] * positions.astype(F32).reshape(1, -1)
    return jnp.cos(ang), jnp.sin(ang)


def kernel(x, positions, even_g_pre, even_g_post, even_w_in, even_b_gate, even_conv_w, even_conv_b,
           even_gn_g, even_gn_b, even_hn_g, even_w_out, odd_g_pre, odd_g_post, odd_w_in, odd_g_q,
           odd_w_uq, odd_g_kv, odd_w_ukv, odd_w_fd, odd_w_out):
    B, S, _ = x.shape
    cos, sin = _rope_tables(positions)
    h = x.reshape(B * S, D_MODEL)
    depth = even_w_in.shape[0] + odd_w_in.shape[0]
    for layer in range(depth):
        j = layer // 2
        if layer % 2 == 0:
            h = _even_layer(h, B, S, even_g_pre[j], even_g_post[j], even_w_in[j], even_b_gate[j],
                            even_conv_w[j], even_conv_b[j], even_gn_g[j], even_gn_b[j], even_hn_g[j],
                            even_w_out[j])
        else:
            h = _odd_layer(h, B, S, cos, sin, odd_g_pre[j], odd_g_post[j], odd_w_in[j], odd_g_q[j],
                           odd_w_uq[j], odd_g_kv[j], odd_w_ukv[j], odd_w_fd[j], odd_w_out[j])
    return h.reshape(B, S, D_MODEL)
```
